```python
import math
import jax
import jax.numpy as jnp
from jax import lax
import numpy as np

D_MODEL = 2048
BATCH = 4
SEQ = 2048
DEPTH = 4
DEC_BATCH = 32
DEC_SEQ = 32
PAST_LEN = 2048

CHUNK = 64
Q_BLOCK = 128
MIX_WIDTH = D_MODEL
W_ATTN = MIX_WIDTH // 2
W_SSM = MIX_WIDTH - W_ATTN
HEAD_DIM = 128
N_HEADS = W_ATTN // HEAD_DIM
SSM_GROUP = 16
N_SSM_GROUPS = W_SSM // SSM_GROUP
SSM_STATE = 64
N_EXPERT_GROUPS = 4
EXPERTS_PER_GROUP = 8
N_EXPERTS = N_EXPERT_GROUPS * EXPERTS_PER_GROUP
TOP_K = 2
D_EXPERT = D_MODEL // 4
MOE_BLOCK = 128
PLE_DIM = 256
ALPHA = (2 * DEPTH) ** 0.25
BETA = (8 * DEPTH) ** -0.25
LN_EPS = 1e-5

kernel_name = 'hymba_sb_s5_hmoe_stream_step'


def layer_norm(x, g, b):
    xf = x.astype(jnp.float32)
    mu = jnp.mean(xf, -1, keepdims=True)
    var = jnp.mean(jnp.square(xf - mu), -1, keepdims=True)
    return ((xf - mu) * lax.rsqrt(var + LN_EPS) * g.astype(jnp.float32) + b.astype(jnp.float32)).astype(x.dtype)


def rms_norm(x, g):
    xf = x.astype(jnp.float32)
    return xf * lax.rsqrt(jnp.mean(jnp.square(xf), -1, keepdims=True) + LN_EPS) * g.astype(jnp.float32)


def sb_attend(q, k, v, q_pos, k_pos):
    z = jnp.einsum('bqhd,bkhd->bhqk', q, k, preferred_element_type=jnp.float32) * (HEAD_DIM ** -0.5)
    visible = k_pos[None, :] < q_pos[:, None]
    log_keep = jnp.where(visible, jax.nn.log_sigmoid(-z), 0.0)
    later = lax.cumsum(log_keep, axis=3, reverse=True) - log_keep
    w = jnp.where(visible, jnp.exp(jax.nn.log_sigmoid(z) + later), 0.0)
    out = jnp.einsum('bhqk,bkhd->bqhd', w.astype(v.dtype), v, preferred_element_type=jnp.float32)
    return out.astype(v.dtype)


def stick_breaking(q, k, v, q_pos, k_pos):
    bsz, tq = q.shape[0], q.shape[1]
    if tq > Q_BLOCK and tq % Q_BLOCK == 0:
        nb = tq // Q_BLOCK
        qb = jnp.moveaxis(q.reshape(bsz, nb, Q_BLOCK, N_HEADS, HEAD_DIM), 1, 0)
        pb = q_pos.reshape(nb, Q_BLOCK)
        ob = lax.map(lambda blk: sb_attend(blk[0], k, v, blk[1], k_pos), (qb, pb))
        return jnp.moveaxis(ob, 0, 1).reshape(bsz, tq, N_HEADS, HEAD_DIM)
    return sb_attend(q, k, v, q_pos, k_pos)


def _linear_combine(e1, e2):
    a1, b1 = e1
    a2, b2 = e2
    return a1 * a2, a2 * b1 + b2


def s5_ssm(u, h0_re, h0_im, lam_re, lam_im, log_step, b_re, b_im, c_re, c_im, d_skip, w_glu, b_glu):
    f32 = jnp.float32
    bsz, L, _ = u.shape
    uf = u.astype(f32).reshape(bsz, L, N_SSM_GROUPS, SSM_GROUP)
    lam = lax.complex(jnp.minimum(lam_re.astype(f32), -1e-4), lam_im.astype(f32))
    step = jnp.exp(log_step.astype(f32))[:, None]
    lam_bar = jnp.exp(lam * step)
    b_bar = ((lam_bar - 1.0) / lam)[:, :, None] * lax.complex(b_re.astype(f32), b_im.astype(f32))
    bu = jnp.einsum('blgc,gpc->blgp', uf.astype(jnp.complex64), b_bar)
    if h0_re is not None:
        h0 = lax.complex(h0_re.astype(f32), h0_im.astype(f32))
        bu = bu.at[:, 0].add(lam_bar * h0)
    a = jnp.broadcast_to(lam_bar, bu.shape)
    _, h = lax.associative_scan(_linear_combine, (a, bu), axis=1)
    c = lax.complex(c_re.astype(f32), c_im.astype(f32))
    y = jnp.einsum('blgp,gcp->blgc', h, c).real + d_skip.astype(f32) * uf
    z = jax.nn.gelu(y.reshape(bsz, L, W_SSM))
    out = z * jax.nn.sigmoid(z @ w_glu.astype(f32) + b_glu.astype(f32))
    return out.astype(u.dtype), h[:, -1].real, h[:, -1].imag


def expert_dispatch(x, expert_ids, gates, w_gate, w_up, w_down):
    T, D = x.shape
    A = expert_ids.shape[0]
    token_ids = jnp.arange(A, dtype=jnp.int32) // TOP_K
    order = jnp.argsort(expert_ids)
    sorted_e = expert_ids[order]
    counts = jnp.bincount(expert_ids, length=N_EXPERTS).astype(jnp.int32)
    padded = (counts + MOE_BLOCK - 1) // MOE_BLOCK * MOE_BLOCK
    start = jnp.cumsum(counts) - counts
    pad_end = jnp.cumsum(padded)
    pad_start = pad_end - padded
    dest = pad_start[sorted_e] + (jnp.arange(A, dtype=jnp.int32) - start[sorted_e])
    nb = (A + MOE_BLOCK - 1) // MOE_BLOCK + N_EXPERTS
    slot_tok = jnp.full((nb * MOE_BLOCK,), T, jnp.int32).at[dest].set(token_ids[order])
    slot_gate = jnp.zeros((nb * MOE_BLOCK,), jnp.float32).at[dest].set(gates[order].astype(jnp.float32))
    block_e = jnp.minimum(jnp.searchsorted(pad_end, jnp.arange(nb, dtype=jnp.int32) * MOE_BLOCK, side='right'), N_EXPERTS - 1)
    xpad = jnp.concatenate([x, jnp.zeros((1, D), x.dtype)], axis=0)
    xb = xpad[slot_tok].reshape(nb, MOE_BLOCK, D)

    def expert_block(args):
        xi, e = args
        hid = jax.nn.silu(xi @ w_gate[e]) * (xi @ w_up[e])
        return hid @ w_down[e]

    yb = lax.map(expert_block, (xb, block_e))
    y = jnp.zeros((T + 1, D), jnp.float32).at[slot_tok].add(yb.reshape(-1, D).astype(jnp.float32) * slot_gate[:, None])
    return y[:T].astype(x.dtype)


def hier_moe(h, w_rg, b_rg, w_re, b_re, w_gate, w_up, w_down):
    bsz, L, D = h.shape
    x = h.reshape(-1, D)
    T = x.shape[0]
    pg = jax.nn.softmax((x @ w_rg + b_rg).astype(jnp.float32), axis=-1)
    pg_top, g_idx = lax.top_k(pg, 1)
    le = (x @ w_re + b_re).astype(jnp.float32).reshape(T, N_EXPERT_GROUPS, EXPERTS_PER_GROUP)
    le_sel = le[jnp.arange(T), g_idx[:, 0]]
    pe_top, e_idx = lax.top_k(jax.nn.softmax(le_sel, axis=-1), TOP_K)
    gates = pg_top * pe_top / jnp.sum(pe_top, -1, keepdims=True)
    experts = g_idx * EXPERTS_PER_GROUP + e_idx
    y = expert_dispatch(x, experts.reshape(-1).astype(jnp.int32), gates.reshape(-1), w_gate, w_up, w_down)
    return y.reshape(bsz, L, D)


def trunk_layer(x, p, k_cache, v_cache, h0_re, h0_im, lw):
    bsz, L, _ = x.shape
    proj = x @ lw['w_in']
    q = proj[..., :W_ATTN].reshape(bsz, L, N_HEADS, HEAD_DIM)
    k = proj[..., W_ATTN:2 * W_ATTN].reshape(bsz, L, N_HEADS, HEAD_DIM)
    v = proj[..., 2 * W_ATTN:3 * W_ATTN].reshape(bsz, L, N_HEADS, HEAD_DIM)
    u = proj[..., 3 * W_ATTN:]
    if k_cache is None:
        past, k_all, v_all = 0, k, v
    else:
        past = k_cache.shape[1]
        k_all = jnp.concatenate([k_cache.astype(k.dtype), k], axis=1)
        v_all = jnp.concatenate([v_cache.astype(v.dtype), v], axis=1)
    q_pos = past + jnp.arange(L, dtype=jnp.int32)
    k_pos = jnp.arange(past + L, dtype=jnp.int32)
    attn = stick_breaking(q, k_all, v_all, q_pos, k_pos).reshape(bsz, L, W_ATTN)
    ssm, h_re, h_im = s5_ssm(u, h0_re, h0_im, lw['ssm_lam_re'], lw['ssm_lam_im'], lw['ssm_log_step'],
                             lw['ssm_b_re'], lw['ssm_b_im'], lw['ssm_c_re'], lw['ssm_c_im'], lw['ssm_d'],
                             lw['ssm_w_glu'], lw['ssm_b_glu'])
    merged = jnp.concatenate([rms_norm(attn, lw['g_attn']), rms_norm(ssm, lw['g_ssm'])], axis=-1).astype(x.dtype)
    h = layer_norm(ALPHA * x + merged @ lw['w_out'], lw['ln1_g'], lw['ln1_b'])
    moe = hier_moe(h, lw['router_w_group'], lw['router_b_group'], lw['router_w_expert'], lw['router_b_expert'],
                   lw['exp_w_gate'], lw['exp_w_up'], lw['exp_w_down'])
    ple = jax.nn.sigmoid(h @ lw['ple_w_gate']) * (p.astype(h.dtype) @ lw['ple_w_proj'])
    x_new = layer_norm(ALPHA * h + moe + ple, lw['ln2_g'], lw['ln2_b'])
    return x_new, k, v, h_re, h_im


def setup_inputs(seed: int = 0) -> dict:
    key = jax.random.key(seed)
    kit = iter([jax.random.fold_in(key, i) for i in range(48)])
    f32 = jnp.float32

    def nrm(shape, scale):
        return jax.random.normal(next(kit), shape, f32) * scale

    def gain(shape):
        return 1.0 + nrm(shape, 0.01)

    D = D_MODEL
    col_scale = jnp.concatenate([jnp.ones((2 * W_ATTN,), f32), jnp.full((W_ATTN,), BETA, f32), jnp.ones((W_SSM,), f32)])
    lam_im = jnp.broadcast_to(jnp.pi * jnp.arange(SSM_STATE, dtype=f32), (DEPTH, N_SSM_GROUPS, SSM_STATE))
    return {
        'x_prompt': nrm((BATCH, SEQ, D), 1.0),
        'x_sample': nrm((DEC_BATCH, DEC_SEQ, D), 1.0),
        'p_prompt': nrm((DEPTH, BATCH, SEQ, PLE_DIM), 1.0),
        'p_sample': nrm((DEPTH, DEC_BATCH, DEC_SEQ, PLE_DIM), 1.0),
        'cache_k': nrm((DEPTH, DEC_BATCH, PAST_LEN, N_HEADS, HEAD_DIM), 1.0),
        'cache_v': nrm((DEPTH, DEC_BATCH, PAST_LEN, N_HEADS, HEAD_DIM), BETA),
        'state_ssm_re': nrm((DEPTH, DEC_BATCH, N_SSM_GROUPS, SSM_STATE), 0.1),
        'state_ssm_im': nrm((DEPTH, DEC_BATCH, N_SSM_GROUPS, SSM_STATE), 0.1),
        'ln_in_g': gain((D,)),
        'ln_in_b': nrm((D,), 0.01),
        'w_in': nrm((DEPTH, D, 3 * W_ATTN + W_SSM), D ** -0.5) * col_scale,
        'ssm_lam_re': -0.5 + nrm((DEPTH, N_SSM_GROUPS, SSM_STATE), 0.01),
        'ssm_lam_im': lam_im + nrm((DEPTH, N_SSM_GROUPS, SSM_STATE), 0.01),
        'ssm_log_step': jax.random.uniform(next(kit), (DEPTH, N_SSM_GROUPS), f32, math.log(1e-3), math.log(1e-1)),
        'ssm_b_re': nrm((DEPTH, N_SSM_GROUPS, SSM_STATE, SSM_GROUP), (2 * SSM_GROUP) ** -0.5),
        'ssm_b_im': nrm((DEPTH, N_SSM_GROUPS, SSM_STATE, SSM_GROUP), (2 * SSM_GROUP) ** -0.5),
        'ssm_c_re': nrm((DEPTH, N_SSM_GROUPS, SSM_GROUP, SSM_STATE), (2 * SSM_STATE) ** -0.5),
        'ssm_c_im': nrm((DEPTH, N_SSM_GROUPS, SSM_GROUP, SSM_STATE), (2 * SSM_STATE) ** -0.5),
        'ssm_d': nrm((DEPTH, N_SSM_GROUPS, SSM_GROUP), 1.0),
        'ssm_w_glu': nrm((DEPTH, W_SSM, W_SSM), W_SSM ** -0.5),
        'ssm_b_glu': nrm((DEPTH, W_SSM), 0.01),
        'g_attn': gain((DEPTH, W_ATTN)),
        'g_ssm': gain((DEPTH, W_SSM)),
        'w_out': nrm((DEPTH, MIX_WIDTH, D), BETA * MIX_WIDTH ** -0.5),
        'ln1_g': gain((DEPTH, D)),
        'ln1_b': nrm((DEPTH, D), 0.01),
        'w_router_group': nrm((DEPTH, D, N_EXPERT_GROUPS), D ** -0.5),
        'b_router_group': nrm((DEPTH, N_EXPERT_GROUPS), 0.01),
        'w_router_expert': nrm((DEPTH, D, N_EXPERTS), D ** -0.5),
        'b_router_expert': nrm((DEPTH, N_EXPERTS), 0.01),
        'w_exp_gate': nrm((DEPTH, N_EXPERTS, D, D_EXPERT), D ** -0.5),
        'w_exp_up': nrm((DEPTH, N_EXPERTS, D, D_EXPERT), D ** -0.5),
        'w_exp_down': nrm((DEPTH, N_EXPERTS, D_EXPERT, D), BETA * D_EXPERT ** -0.5),
        'w_ple_gate': nrm((DEPTH, D, D), D ** -0.5),
        'w_ple_proj': nrm((DEPTH, PLE_DIM, D), BETA * PLE_DIM ** -0.5),
        'ln2_g': gain((DEPTH, D)),
        'ln2_b': nrm((DEPTH, D), 0.01),
    }


def reference(x_prompt, x_sample, p_prompt, p_sample, cache_k, cache_v, state_ssm_re, state_ssm_im,
              ln_in_g, ln_in_b, w_in, ssm_lam_re, ssm_lam_im, ssm_log_step, ssm_b_re, ssm_b_im,
              ssm_c_re, ssm_c_im, ssm_d, ssm_w_glu, ssm_b_glu, g_attn, g_ssm, w_out, ln1_g, ln1_b,
              w_router_group, b_router_group, w_router_expert, b_router_expert,
              w_exp_gate, w_exp_up, w_exp_down, w_ple_gate, w_ple_proj, ln2_g, ln2_b):
    xp = layer_norm(x_prompt, ln_in_g, ln_in_b)
    xs = layer_norm(x_sample, ln_in_g, ln_in_b)
    kp_l, vp_l, hrp_l, hip_l = [], [], [], []
    ks_l, vs_l, hrs_l, his_l = [], [], [], []
    for l in range(DEPTH):
        lw = {
            'w_in': w_in[l],
            'ssm_lam_re': ssm_lam_re[l], 'ssm_lam_im': ssm_lam_im[l], 'ssm_log_step': ssm_log_step[l],
            'ssm_b_re': ssm_b_re[l], 'ssm_b_im': ssm_b_im[l], 'ssm_c_re': ssm_c_re[l], 'ssm_c_im': ssm_c_im[l],
            'ssm_d': ssm_d[l], 'ssm_w_glu': ssm_w_glu[l], 'ssm_b_glu': ssm_b_glu[l],
            'g_attn': g_attn[l], 'g_ssm': g_ssm[l], 'w_out': w_out[l], 'ln1_g': ln1_g[l], 'ln1_b': ln1_b[l],
            'router_w_group': w_router_group[l], 'router_b_group': b_router_group[l],
            'router_w_expert': w_router_expert[l], 'router_b_expert': b_router_expert[l],
            'exp_w_gate': w_exp_gate[l], 'exp_w_up': w_exp_up[l], 'exp_w_down': w_exp_down[l],
            'ple_w_gate': w_ple_gate[l], 'ple_w_proj': w_ple_proj[l], 'ln2_g': ln2_g[l], 'ln2_b': ln2_b[l],
        }
        xp, kp, vp, hrp, hip = trunk_layer(xp, p_prompt[l], None, None, None, None, lw)
        xs, ks, vs, hrs, his = trunk_layer(xs, p_sample[l], cache_k[l], cache_v[l], state_ssm_re[l], state_ssm_im[l], lw)
        kp_l.append(kp); vp_l.append(vp); hrp_l.append(hrp); hip_l.append(hip)
        ks_l.append(ks); vs_l.append(vs); hrs_l.append(hrs); his_l.append(his)
    k_prompt = jnp.stack(kp_l)
    v_prompt = jnp.stack(vp_l)
    ssm_re_prompt = jnp.stack(hrp_l)
    ssm_im_prompt = jnp.stack(hip_l)
    k_sample = jnp.stack(ks_l)
    v_sample = jnp.stack(vs_l)
    ssm_re_sample = jnp.stack(hrs_l)
    ssm_im_sample = jnp.stack(his_l)
    return (xp, xs, k_prompt, v_prompt, ssm_re_prompt, ssm_im_prompt, k_sample, v_sample, ssm_re_sample, ssm_im_sample)
```

```python
import functools
import math

import jax
import jax.numpy as jnp
from jax import lax
from jax.experimental import pallas as pl
from jax.experimental.pallas import tpu as pltpu

F32 = jnp.float32
BF16 = jnp.bfloat16

D_MODEL = 2048
DEPTH = 4
W_ATTN = 1024
W_SSM = 1024
HEAD_DIM = 128
N_HEADS = 8
SSM_GROUP = 16
N_SSM_GROUPS = 64
SSM_STATE = 64
N_EXPERT_GROUPS = 4
EXPERTS_PER_GROUP = 8
N_EXPERTS = 32
TOP_K = 2
D_EXPERT = 512
PLE_DIM = 256
ALPHA = (2 * DEPTH) ** 0.25
LN_EPS = 1e-5

LANES = 128
SUBLANES = 8
VMEM_LIMIT = 56 * 1024 * 1024
SSM_LANES = N_SSM_GROUPS * SSM_STATE
SSM_PIECES = 8
ATT_BLK = 128
MOE_ROWS = 256


def _cparams(*sem):
    return pltpu.CompilerParams(dimension_semantics=sem, vmem_limit_bytes=VMEM_LIMIT)


def _dot(a, b):
    return jnp.dot(a, b, preferred_element_type=F32)


def _dot_nt(a, b):
    return lax.dot_general(a, b, (((1,), (1,)), ((), ())), preferred_element_type=F32)


def _split_bf16(x):
    hi = x.astype(BF16)
    lo = (x - hi.astype(F32)).astype(BF16)
    return hi, lo


def _layer_norm(x, g, b):
    mu = jnp.mean(x, -1, keepdims=True)
    xc = x - mu
    var = jnp.mean(xc * xc, -1, keepdims=True)
    return xc * lax.rsqrt(var + LN_EPS) * g + b


def _rms_norm(x, g):
    return x * lax.rsqrt(jnp.mean(x * x, -1, keepdims=True) + LN_EPS) * g


def _sigmoid(x):
    return 1.0 / (1.0 + jnp.exp(-x))


def _softplus(z):
    return jnp.maximum(z, 0.0) + jnp.log1p(jnp.exp(-jnp.abs(z)))


def _ln_in_kernel(x_ref, g_ref, b_ref, o_ref, ob_ref):
    y = _layer_norm(x_ref[...], g_ref[...], b_ref[...])
    o_ref[...] = y
    ob_ref[...] = y.astype(BF16)


def ln_in(x, g, b, tm=256):
    T, D = x.shape
    row = pl.BlockSpec((tm, D), lambda i: (i, 0))
    vec = pl.BlockSpec((1, D), lambda i: (0, 0))
    return pl.pallas_call(
        _ln_in_kernel, grid=(T // tm,), in_specs=[row, vec, vec], out_specs=[row, row],
        out_shape=[jax.ShapeDtypeStruct((T, D), F32), jax.ShapeDtypeStruct((T, D), BF16)],
        compiler_params=_cparams("parallel"), name="ln_in")(x, g.reshape(1, D), b.reshape(1, D))


def _matmul_kernel(x_ref, w_ref, o_ref):
    o_ref[...] = _dot(x_ref[...], w_ref[...])


def matmul(x, w, tm=1024, tn=512):
    M, K = x.shape
    N = w.shape[1]
    return pl.pallas_call(
        _matmul_kernel, grid=(M // tm, N // tn),
        in_specs=[pl.BlockSpec((tm, K), lambda i, j: (i, 0)), pl.BlockSpec((K, tn), lambda i, j: (0, j))],
        out_specs=pl.BlockSpec((tm, tn), lambda i, j: (i, j)),
        out_shape=jax.ShapeDtypeStruct((M, N), F32),
        compiler_params=_cparams("parallel", "arbitrary"), name="w_in_proj")(x, w)


def _sb_block(qh, kblk, vblk, run, acc, strict_u, visible):
    z = _dot_nt(qh, kblk) * (HEAD_DIM ** -0.5)
    log_keep = -_softplus(z)
    if visible is not None:
        log_keep = jnp.where(visible, log_keep, 0.0)
    hi, lo = _split_bf16(log_keep)
    later = _dot(hi, strict_u) + _dot(lo, strict_u)
    w = jnp.exp(z + log_keep + later + run)
    if visible is not None:
        w = jnp.where(visible, w, 0.0)
    acc = acc + _dot(w.astype(BF16), vblk)
    run = run + jnp.sum(log_keep, -1, keepdims=True)
    return run, acc


def _strict_upper(n):
    r = lax.broadcasted_iota(jnp.int32, (n, n), 0)
    c = lax.broadcasted_iota(jnp.int32, (n, n), 1)
    return jnp.where(r > c, 1.0, 0.0).astype(BF16), c < r


def _attn_prompt_kernel(q_ref, k_ref, v_ref, o_ref, kb_ref, vb_ref):
    qi = pl.program_id(2)

    @pl.when(qi == 0)
    def _():
        kb_ref[...] = k_ref[...].astype(BF16)
        vb_ref[...] = v_ref[...].astype(BF16)

    strict_u, visible = _strict_upper(ATT_BLK)
    qh = q_ref[...].astype(BF16)
    diag = pl.multiple_of(qi * ATT_BLK, ATT_BLK)
    run = jnp.zeros((ATT_BLK, 1), F32)
    acc = jnp.zeros((ATT_BLK, HEAD_DIM), F32)
    run, acc = _sb_block(qh, kb_ref[pl.ds(diag, ATT_BLK), :], vb_ref[pl.ds(diag, ATT_BLK), :],
                         run, acc, strict_u, visible)

    def body(it, carry):
        start = pl.multiple_of((qi - 1 - it) * ATT_BLK, ATT_BLK)
        return _sb_block(qh, kb_ref[pl.ds(start, ATT_BLK), :], vb_ref[pl.ds(start, ATT_BLK), :],
                         carry[0], carry[1], strict_u, None)

    run, acc = lax.fori_loop(0, qi, body, (run, acc))
    o_ref[...] = acc


def attn_prompt(proj, batch, seq):
    nq = seq // ATT_BLK
    q_spec = pl.BlockSpec((ATT_BLK, HEAD_DIM), lambda b, h, i: (b * nq + i, h))
    k_spec = pl.BlockSpec((seq, HEAD_DIM), lambda b, h, i: (b, N_HEADS + h))
    v_spec = pl.BlockSpec((seq, HEAD_DIM), lambda b, h, i: (b, 2 * N_HEADS + h))
    return pl.pallas_call(
        _attn_prompt_kernel, grid=(batch, N_HEADS, nq), in_specs=[q_spec, k_spec, v_spec],
        out_specs=q_spec, out_shape=jax.ShapeDtypeStruct((batch * seq, W_ATTN), F32),
        scratch_shapes=[pltpu.VMEM((seq, HEAD_DIM), BF16), pltpu.VMEM((seq, HEAD_DIM), BF16)],
        compiler_params=_cparams("parallel", "parallel", "arbitrary"), name="attn_prompt")(proj, proj, proj)


def _attn_sample_kernel(q_ref, k_ref, v_ref, ck_ref, cv_ref, o_ref, *, past, tq):
    strict_new, visible_new = _strict_upper(tq)
    strict_u, _ = _strict_upper(ATT_BLK)
    for h in range(N_HEADS):
        cols = slice(h * HEAD_DIM, (h + 1) * HEAD_DIM)
        qh = q_ref[:, cols].astype(BF16)
        run = jnp.zeros((tq, 1), F32)
        acc = jnp.zeros((tq, HEAD_DIM), F32)
        run, acc = _sb_block(qh, k_ref[:, cols].astype(BF16), v_ref[:, cols].astype(BF16),
                             run, acc, strict_new, visible_new)

        def body(it, carry, cols=cols, qh=qh):
            start = pl.multiple_of(past - (it + 1) * ATT_BLK, ATT_BLK)
            kblk = ck_ref[0, pl.ds(start, ATT_BLK), cols].astype(BF16)
            vblk = cv_ref[0, pl.ds(start, ATT_BLK), cols].astype(BF16)
            return _sb_block(qh, kblk, vblk, carry[0], carry[1], strict_u, None)

        run, acc = lax.fori_loop(0, past // ATT_BLK, body, (run, acc))
        o_ref[:, cols] = acc


def attn_sample(proj, cache_k, cache_v, layer, row0, batch, tq):
    past = cache_k.shape[1]
    rb0 = row0 // tq
    new = lambda c: pl.BlockSpec((tq, W_ATTN), lambda b: (rb0 + b, c))
    cache = pl.BlockSpec((1, past, W_ATTN), lambda b: (layer * batch + b, 0, 0))
    return pl.pallas_call(
        functools.partial(_attn_sample_kernel, past=past, tq=tq), grid=(batch,),
        in_specs=[new(0), new(1), new(2), cache, cache],
        out_specs=pl.BlockSpec((tq, W_ATTN), lambda b: (b, 0)),
        out_shape=jax.ShapeDtypeStruct((batch * tq, W_ATTN), F32),
        compiler_params=_cparams("parallel"), name="attn_sample")(proj, proj, proj, cache_k, cache_v)


def _ssm_param_kernel(lr_ref, li_ref, ls_ref, br_ref, bi_ref, ar_ref, ai_ref, bbr_ref, bbi_ref):
    lam_re = jnp.minimum(lr_ref[0], -1e-4)
    lam_im = li_ref[0]
    step = jnp.exp(ls_ref[0])
    mag = jnp.exp(lam_re * step)
    a_re = mag * jnp.cos(lam_im * step)
    a_im = mag * jnp.sin(lam_im * step)
    ar_ref[0] = a_re
    ai_ref[0] = a_im
    den = lam_re * lam_re + lam_im * lam_im
    nr = a_re - 1.0
    c_re = (nr * lam_re + a_im * lam_im) / den
    c_im = (a_im * lam_re - nr * lam_im) / den
    for c in range(SSM_GROUP):
        b_re = br_ref[0, c]
        b_im = bi_ref[0, c]
        bbr_ref[0, c] = c_re * b_re - c_im * b_im
        bbi_ref[0, c] = c_re * b_im + c_im * b_re


def ssm_params(lam_re, lam_im, log_step, b_re, b_im):
    G, P, C = N_SSM_GROUPS, SSM_STATE, SSM_GROUP
    gp = pl.BlockSpec((1, G, P), lambda l: (l, 0, 0))
    cgp = pl.BlockSpec((1, C, G, P), lambda l: (l, 0, 0, 0))
    return pl.pallas_call(
        _ssm_param_kernel, grid=(DEPTH,),
        in_specs=[gp, gp, pl.BlockSpec((1, G, 1), lambda l: (l, 0, 0)), cgp, cgp],
        out_specs=[gp, gp, cgp, cgp],
        out_shape=[jax.ShapeDtypeStruct((DEPTH, G, P), F32)] * 2 + [jax.ShapeDtypeStruct((DEPTH, C, G, P), F32)] * 2,
        compiler_params=_cparams("parallel"), name="ssm_params")(
            lam_re, lam_im, log_step.reshape(DEPTH, G, 1),
            b_re.transpose(0, 3, 1, 2), b_im.transpose(0, 3, 1, 2))


def _gelu_tanh(y):
    return 0.5 * y * (1.0 + jnp.tanh(math.sqrt(2.0 / math.pi) * (y + 0.044715 * (y * y * y))))


def _ssm_kernel(u_ref, h0r_ref, h0i_ref, ar_ref, ai_ref, wb_ref, wcr_ref, wci_ref, d_ref,
                z_ref, sr_ref, si_ref, hr_ref, hi_ref, wbh_ref, wbl_ref, *, rows, steps, lane_blk):
    t = pl.program_id(0)
    piece_in = SSM_LANES // SSM_PIECES
    piece_ch = W_SSM // SSM_PIECES

    @pl.when(t == 0)
    def _():
        sr_ref[...] = h0r_ref[...]
        si_ref[...] = h0i_ref[...]
        w = wb_ref[...]
        hi = w.astype(BF16)
        wbh_ref[...] = hi
        wbl_ref[...] = (w - hi.astype(F32)).astype(BF16)

    for k in range(SSM_PIECES):
        uk = u_ref[:, k * piece_ch:(k + 1) * piece_ch]
        uh, ul = _split_bf16(uk)
        bu = _dot(uh, wbh_ref[k]) + _dot(uh, wbl_ref[k]) + _dot(ul, wbh_ref[k])
        hr_ref[:, k * piece_in:(k + 1) * piece_in] = bu[:, :piece_in]
        hi_ref[:, k * piece_in:(k + 1) * piece_in] = bu[:, piece_in:]

    for lb in range(SSM_LANES // lane_blk):
        ln = slice(lb * lane_blk, (lb + 1) * lane_blk)
        a_re = jnp.broadcast_to(ar_ref[:, ln], (rows, lane_blk))
        a_im = jnp.broadcast_to(ai_ref[:, ln], (rows, lane_blk))

        def step(j, carry, ln=ln, a_re=a_re, a_im=a_im):
            h_re, h_im = carry
            r0 = pl.multiple_of(j * rows, rows)
            n_re = a_re * h_re - a_im * h_im + hr_ref[pl.ds(r0, rows), ln]
            n_im = a_re * h_im + a_im * h_re + hi_ref[pl.ds(r0, rows), ln]
            hr_ref[pl.ds(r0, rows), ln] = n_re
            hi_ref[pl.ds(r0, rows), ln] = n_im
            return n_re, n_im

        h_re, h_im = lax.fori_loop(0, steps, step, (sr_ref[:, ln], si_ref[:, ln]), unroll=4)
        sr_ref[:, ln] = h_re
        si_ref[:, ln] = h_im

    for k in range(SSM_PIECES):
        ln = slice(k * piece_in, (k + 1) * piece_in)
        ch = slice(k * piece_ch, (k + 1) * piece_ch)
        y = (_dot(hr_ref[:, ln].astype(BF16), wcr_ref[k].astype(BF16))
             - _dot(hi_ref[:, ln].astype(BF16), wci_ref[k].astype(BF16)))
        y = y + d_ref[:, ch] * u_ref[:, ch]
        z_ref[:, ch] = _gelu_tanh(y)


def ssm_scan(u_tm, h0_re, h0_im, a_re, a_im, wb, wc_re, wc_im, d, rows, steps):
    n = u_tm.shape[0] // (rows * steps)
    tile = rows * steps
    lane_blk = 4 * SUBLANES * LANES // rows
    full = lambda *s: pl.BlockSpec(s, lambda t: (0,) * len(s))
    return pl.pallas_call(
        functools.partial(_ssm_kernel, rows=rows, steps=steps, lane_blk=lane_blk), grid=(n,),
        in_specs=[pl.BlockSpec((tile, W_SSM), lambda t: (t, 0)),
                  full(rows, SSM_LANES), full(rows, SSM_LANES), full(1, SSM_LANES), full(1, SSM_LANES),
                  full(*wb.shape), full(*wc_re.shape), full(*wc_im.shape), full(1, W_SSM)],
        out_specs=[pl.BlockSpec((tile, W_SSM), lambda t: (t, 0)), full(rows, SSM_LANES), full(rows, SSM_LANES)],
        out_shape=[jax.ShapeDtypeStruct(u_tm.shape, F32)] + [jax.ShapeDtypeStruct((rows, SSM_LANES), F32)] * 2,
        scratch_shapes=[pltpu.VMEM((tile, SSM_LANES), F32), pltpu.VMEM((tile, SSM_LANES), F32),
                        pltpu.VMEM(wb.shape, BF16), pltpu.VMEM(wb.shape, BF16)],
        compiler_params=_cparams("arbitrary"), name="ssm_scan")(
            u_tm, h0_re, h0_im, a_re, a_im, wb, wc_re, wc_im, d)


def _block_diag(w):
    k, g, a, b = w.shape
    eye = jnp.eye(g, dtype=w.dtype)
    return (w[:, :, :, None, :] * eye[None, :, None, :, None]).reshape(k, g * a, g * b)


def ssm_weights(bb_re, bb_im, c_re, c_im):
    gpp = N_SSM_GROUPS // SSM_PIECES
    to_in = lambda b: b.transpose(1, 0, 2).reshape(SSM_PIECES, gpp, SSM_GROUP, SSM_STATE)
    wb = jnp.concatenate([_block_diag(to_in(bb_re)), _block_diag(to_in(bb_im))], axis=-1)
    to_out = lambda c: c.transpose(0, 2, 1).reshape(SSM_PIECES, gpp, SSM_STATE, SSM_GROUP)
    return wb, _block_diag(to_out(c_re)), _block_diag(to_out(c_im))


def _glu_kernel(z_ref, w_ref, b_ref, o_ref):
    z = z_ref[...]
    o_ref[...] = z * _sigmoid(_dot(z.astype(BF16), w_ref[...]) + b_ref[...])


def glu(z, w, b, tm=1024):
    M, N = z.shape
    row = pl.BlockSpec((tm, N), lambda i: (i, 0))
    return pl.pallas_call(
        _glu_kernel, grid=(M // tm,),
        in_specs=[row, pl.BlockSpec((N, N), lambda i: (0, 0)), pl.BlockSpec((1, N), lambda i: (0, 0))],
        out_specs=row, out_shape=jax.ShapeDtypeStruct((M, N), F32),
        compiler_params=_cparams("parallel"), name="ssm_glu")(z, w, b.reshape(1, N))


def _mix_out_kernel(a_ref, s_ref, x_ref, ga_ref, gs_ref, w_ref, g_ref, b_ref, h_ref, hb_ref):
    a = _rms_norm(a_ref[...], ga_ref[...]).astype(BF16)
    s = _rms_norm(s_ref[...], gs_ref[...]).astype(BF16)
    y = _dot(a, w_ref[:W_ATTN, :]) + _dot(s, w_ref[W_ATTN:, :])
    h = _layer_norm(ALPHA * x_ref[...] + y, g_ref[...], b_ref[...])
    h_ref[...] = h
    hb_ref[...] = h.astype(BF16)


def mix_out(attn, ssm, x, g_attn, g_ssm, w_out, ln_g, ln_b, tm=256):
    T, D = x.shape
    half = pl.BlockSpec((tm, W_ATTN), lambda i: (i, 0))
    row = pl.BlockSpec((tm, D), lambda i: (i, 0))
    vec = lambda n: pl.BlockSpec((1, n), lambda i: (0, 0))
    return pl.pallas_call(
        _mix_out_kernel, grid=(T // tm,),
        in_specs=[half, half, row, vec(W_ATTN), vec(W_SSM), pl.BlockSpec((D, D), lambda i: (0, 0)), vec(D), vec(D)],
        out_specs=[row, row],
        out_shape=[jax.ShapeDtypeStruct((T, D), F32), jax.ShapeDtypeStruct((T, D), BF16)],
        compiler_params=_cparams("parallel"), name="mix_out_ln1")(
            attn, ssm, x, g_attn.reshape(1, -1), g_ssm.reshape(1, -1), w_out, ln_g.reshape(1, D), ln_b.reshape(1, D))


def _router_kernel(h_ref, w_ref, b_ref, o_ref):
    tm = h_ref.shape[0]
    hh, hl = _split_bf16(h_ref[...])
    wh, wl = _split_bf16(w_ref[...])
    logits = _dot(hh, wh) + _dot(hh, wl) + _dot(hl, wh) + b_ref[...]
    lane = lax.broadcasted_iota(jnp.int32, (tm, LANES), 1).astype(F32)
    neg = -jnp.inf
    big = float(LANES)

    def first_max(v):
        m = jnp.max(v, -1, keepdims=True)
        return m, jnp.min(jnp.where(v == m, lane, big), -1, keepdims=True)

    is_group = lane < N_EXPERT_GROUPS
    g_max, g_idx = first_max(jnp.where(is_group, logits, neg))
    pg_top = 1.0 / jnp.sum(jnp.where(is_group, jnp.exp(logits - g_max), 0.0), -1, keepdims=True)
    lo = N_EXPERT_GROUPS + EXPERTS_PER_GROUP * g_idx
    le = jnp.where(lane >= lo, jnp.where(lane < lo + EXPERTS_PER_GROUP, logits, neg), neg)
    m1, i1 = first_max(le)
    m2, i2 = first_max(jnp.where(lane == i1, neg, le))
    e2 = jnp.exp(m2 - m1)
    den = 1.0 + e2
    o_ref[...] = jnp.where(lane == 0, i1 - N_EXPERT_GROUPS,
                 jnp.where(lane == 1, i2 - N_EXPERT_GROUPS,
                 jnp.where(lane == 2, pg_top / den,
                 jnp.where(lane == 3, pg_top * e2 / den, 0.0))))


def router(h, w_cat, b_cat, tm=512):
    T, D = h.shape
    return pl.pallas_call(
        _router_kernel, grid=(T // tm,),
        in_specs=[pl.BlockSpec((tm, D), lambda i: (i, 0)), pl.BlockSpec((D, LANES), lambda i: (0, 0)),
                  pl.BlockSpec((1, LANES), lambda i: (0, 0))],
        out_specs=pl.BlockSpec((tm, LANES), lambda i: (i, 0)),
        out_shape=jax.ShapeDtypeStruct((T, LANES), F32),
        compiler_params=_cparams("parallel"), name="router")(h, w_cat, b_cat)


def dispatch_plan(expert_ids, n_blocks):
    flat = expert_ids.reshape(-1)
    A = flat.shape[0]
    onehot = (flat[:, None] == jnp.arange(N_EXPERTS, dtype=jnp.int32)[None, :]).astype(jnp.int32)
    rank = jnp.sum((jnp.cumsum(onehot, axis=0) - 1) * onehot, axis=1)
    counts = jnp.sum(onehot, axis=0)
    blocks = (counts + MOE_ROWS - 1) // MOE_ROWS
    blk_end = jnp.cumsum(blocks)
    pad_start = (blk_end - blocks) * MOE_ROWS
    pos = pad_start[flat] + rank
    slot_tok = jnp.zeros((n_blocks * MOE_ROWS,), jnp.int32).at[pos].set(jnp.arange(A, dtype=jnp.int32) // TOP_K)
    n_used = blk_end[-1]
    blk = jnp.minimum(jnp.arange(n_blocks, dtype=jnp.int32), n_used - 1)
    block_expert = jnp.minimum(jnp.searchsorted(blk_end, blk, side='right'), N_EXPERTS - 1).astype(jnp.int32)
    return slot_tok, pos.astype(jnp.int32), block_expert, n_used.reshape(1).astype(jnp.int32)


def _row_copy(src_ref, dst_ref, src_row, dst_row, sem):
    return pltpu.make_async_copy(src_ref.at[pl.ds(src_row, 1)], dst_ref.at[pl.ds(dst_row, 1)], sem)


def _gather_rows_kernel(tok_ref, src_ref, o_ref, sem):
    rows = o_ref.shape[0]
    base = pl.program_id(0) * rows

    def start(r, c):
        _row_copy(src_ref, o_ref, tok_ref[base + r], r, sem).start()
        return c

    def wait(r, c):
        _row_copy(src_ref, o_ref, 0, r, sem).wait()
        return c

    lax.fori_loop(0, rows, start, 0)
    lax.fori_loop(0, rows, wait, 0)


def gather_rows(src, slot_tok, rows=MOE_ROWS):
    n = slot_tok.shape[0]
    D = src.shape[1]
    return pl.pallas_call(
        _gather_rows_kernel,
        grid_spec=pltpu.PrefetchScalarGridSpec(
            num_scalar_prefetch=1, grid=(n // rows,),
            in_specs=[pl.BlockSpec(memory_space=pl.ANY)],
            out_specs=pl.BlockSpec((rows, D), lambda i, tok: (i, 0)),
            scratch_shapes=[pltpu.SemaphoreType.DMA(())]),
        out_shape=jax.ShapeDtypeStruct((n, D), src.dtype),
        compiler_params=_cparams("arbitrary"), name="moe_gather")(slot_tok, src)


def _expert_kernel(be_ref, nu_ref, x_ref, wg_ref, wu_ref, wd_ref, o_ref, wgb_ref, wub_ref, wdb_ref):
    i = pl.program_id(0)
    changed = jnp.logical_or(i == 0, be_ref[i] != be_ref[jnp.maximum(i - 1, 0)])

    @pl.when(changed)
    def _():
        wgb_ref[...] = wg_ref[0].astype(BF16)
        wub_ref[...] = wu_ref[0].astype(BF16)
        wdb_ref[...] = wd_ref[0].astype(BF16)

    @pl.when(i < nu_ref[0])
    def _():
        x = x_ref[...].astype(BF16)
        g = _dot(x, wgb_ref[...])
        u = _dot(x, wub_ref[...])
        hid = (g * _sigmoid(g)) * u
        o_ref[...] = _dot(hid.astype(BF16), wdb_ref[...])

    @pl.when(i >= nu_ref[0])
    def _():
        o_ref[...] = jnp.zeros_like(o_ref)


def expert_mlp(xs, block_expert, n_used, w_gate, w_up, w_down, layer):
    n, D = xs.shape
    nb = n // MOE_ROWS
    return pl.pallas_call(
        _expert_kernel,
        grid_spec=pltpu.PrefetchScalarGridSpec(
            num_scalar_prefetch=2, grid=(nb,),
            in_specs=[pl.BlockSpec((MOE_ROWS, D), lambda i, be, nu: (jnp.minimum(i, nu[0] - 1), 0)),
                      pl.BlockSpec((1, D, D_EXPERT), lambda i, be, nu: (layer * N_EXPERTS + be[i], 0, 0)),
                      pl.BlockSpec((1, D, D_EXPERT), lambda i, be, nu: (layer * N_EXPERTS + be[i], 0, 0)),
                      pl.BlockSpec((1, D_EXPERT, D), lambda i, be, nu: (layer * N_EXPERTS + be[i], 0, 0))],
            out_specs=pl.BlockSpec((MOE_ROWS, D), lambda i, be, nu: (i, 0)),
            scratch_shapes=[pltpu.VMEM((D, D_EXPERT), BF16), pltpu.VMEM((D, D_EXPERT), BF16),
                            pltpu.VMEM((D_EXPERT, D), BF16)]),
        out_shape=jax.ShapeDtypeStruct((n, D), F32),
        compiler_params=_cparams("arbitrary"), name="moe_experts")(block_expert, n_used, xs, w_gate, w_up, w_down)


def _layer_out_kernel(pos_ref, h_ref, hb_ref, p_ref, r_ref, yb_ref, wg_ref, wp_ref, g_ref, b_ref,
                      o_ref, ob_ref, buf_ref, sem):
    tm = h_ref.shape[0]
    base = pl.program_id(0) * tm

    def start(r, c):
        for k in range(TOP_K):
            _row_copy(yb_ref, buf_ref.at[k], pos_ref[(base + r) * TOP_K + k], r, sem).start()
        return c

    def wait(r, c):
        for k in range(TOP_K):
            _row_copy(yb_ref, buf_ref.at[k], 0, r, sem).wait()
        return c

    lax.fori_loop(0, tm, start, 0)
    ple = _sigmoid(_dot(hb_ref[...], wg_ref[...])) * _dot(p_ref[...].astype(BF16), wp_ref[...])
    lax.fori_loop(0, tm, wait, 0)
    route = r_ref[...]
    moe = route[:, 2:3] * buf_ref[0] + route[:, 3:4] * buf_ref[1]
    x = _layer_norm(ALPHA * h_ref[...] + moe + ple, g_ref[...], b_ref[...])
    o_ref[...] = x
    ob_ref[...] = x.astype(BF16)


def layer_out(h, hb, p, route, pos, yb, w_ple_gate, w_ple_proj, ln_g, ln_b, tm=256):
    T, D = h.shape
    row = lambda n: pl.BlockSpec((tm, n), lambda i, pos: (i, 0))
    full = lambda a, b: pl.BlockSpec((a, b), lambda i, pos: (0, 0))
    return pl.pallas_call(
        _layer_out_kernel,
        grid_spec=pltpu.PrefetchScalarGridSpec(
            num_scalar_prefetch=1, grid=(T // tm,),
            in_specs=[row(D), row(D), row(PLE_DIM), row(LANES), pl.BlockSpec(memory_space=pl.ANY),
                      full(D, D), full(PLE_DIM, D), full(1, D), full(1, D)],
            out_specs=[row(D), row(D)],
            scratch_shapes=[pltpu.VMEM((TOP_K, tm, D), F32), pltpu.SemaphoreType.DMA(())]),
        out_shape=[jax.ShapeDtypeStruct((T, D), F32), jax.ShapeDtypeStruct((T, D), BF16)],
        compiler_params=_cparams("arbitrary"), name="moe_combine_ple_ln2")(
            pos, h, hb, p, route, yb, w_ple_gate, w_ple_proj, ln_g.reshape(1, D), ln_b.reshape(1, D))


def kernel(x_prompt, x_sample, p_prompt, p_sample, cache_k, cache_v, state_ssm_re, state_ssm_im, ln_in_g, ln_in_b, w_in, ssm_lam_re, ssm_lam_im, ssm_log_step, ssm_b_re, ssm_b_im, ssm_c_re, ssm_c_im, ssm_d, ssm_w_glu, ssm_b_glu, g_attn, g_ssm, w_out, ln1_g, ln1_b, w_router_group, b_router_group, w_router_expert, b_router_expert, w_exp_gate, w_exp_up, w_exp_down, w_ple_gate, w_ple_proj, ln2_g, ln2_b):
    B, L, D = x_prompt.shape
    SB, SL, _ = x_sample.shape
    depth = w_in.shape[0]
    past = cache_k.shape[2]
    TP, TS = B * L, SB * SL
    T = TP + TS
    n_blocks = (T * TOP_K) // MOE_ROWS + N_EXPERTS

    x, xb = ln_in(jnp.concatenate([x_prompt.reshape(TP, D), x_sample.reshape(TS, D)], axis=0), ln_in_g, ln_in_b)
    p_all = jnp.concatenate([p_prompt.reshape(depth, TP, PLE_DIM), p_sample.reshape(depth, TS, PLE_DIM)], axis=1)

    w_in_b = w_in.astype(BF16)
    w_out_b = w_out.astype(BF16)
    w_glu_b = ssm_w_glu.astype(BF16)
    w_pg_b = w_ple_gate.astype(BF16)
    w_pp_b = w_ple_proj.astype(BF16)
    w_route = jnp.pad(jnp.concatenate([w_router_group, w_router_expert], axis=-1),
                      ((0, 0), (0, 0), (0, LANES - N_EXPERT_GROUPS - N_EXPERTS)))
    b_route = jnp.pad(jnp.concatenate([b_router_group, b_router_expert], axis=-1),
                      ((0, 0), (0, LANES - N_EXPERT_GROUPS - N_EXPERTS))).reshape(depth, 1, LANES)
    a_re, a_im, bb_re, bb_im = ssm_params(ssm_lam_re, ssm_lam_im, ssm_log_step, ssm_b_re, ssm_b_im)

    cache_k_all = cache_k.reshape(depth * SB, past, W_ATTN)
    cache_v_all = cache_v.reshape(depth * SB, past, W_ATTN)
    w_eg = w_exp_gate.reshape(depth * N_EXPERTS, D, D_EXPERT)
    w_eu = w_exp_up.reshape(depth * N_EXPERTS, D, D_EXPERT)
    w_ed = w_exp_down.reshape(depth * N_EXPERTS, D_EXPERT, D)
    pad_rows = SUBLANES - B
    zeros_state = jnp.zeros((SUBLANES, SSM_LANES), F32)
    outs = [[] for _ in range(8)]
    for l in range(depth):
        proj = matmul(xb, w_in_b[l])
        attn_p = attn_prompt(proj, B, L)
        attn_s = attn_sample(proj, cache_k_all, cache_v_all, l, TP, SB, SL)

        wb, wc_re, wc_im = ssm_weights(bb_re[l], bb_im[l], ssm_c_re[l], ssm_c_im[l])
        lam_r = a_re[l].reshape(1, SSM_LANES)
        lam_i = a_im[l].reshape(1, SSM_LANES)
        d_row = ssm_d[l].reshape(1, W_SSM)
        u_p = proj[:TP, 3 * W_ATTN:].reshape(B, L, W_SSM).transpose(1, 0, 2)
        u_p = jnp.pad(u_p, ((0, 0), (0, pad_rows), (0, 0))).reshape(L * SUBLANES, W_SSM)
        z_p, hr_p, hi_p = ssm_scan(u_p, zeros_state, zeros_state, lam_r, lam_i, wb, wc_re, wc_im, d_row,
                                   rows=SUBLANES, steps=32)
        z_p = z_p.reshape(L, SUBLANES, W_SSM)[:, :B].transpose(1, 0, 2).reshape(TP, W_SSM)
        u_s = proj[TP:, 3 * W_ATTN:].reshape(SB, SL, W_SSM).transpose(1, 0, 2).reshape(TS, W_SSM)
        z_s, hr_s, hi_s = ssm_scan(u_s, state_ssm_re[l].reshape(SB, SSM_LANES), state_ssm_im[l].reshape(SB, SSM_LANES),
                                   lam_r, lam_i, wb, wc_re, wc_im, d_row, rows=SB, steps=8)
        z_s = z_s.reshape(SL, SB, W_SSM).transpose(1, 0, 2).reshape(TS, W_SSM)
        ssm = glu(jnp.concatenate([z_p, z_s], axis=0), w_glu_b[l], ssm_b_glu[l])

        h, hb = mix_out(jnp.concatenate([attn_p, attn_s], axis=0), ssm, x, g_attn[l], g_ssm[l], w_out_b[l],
                        ln1_g[l], ln1_b[l])
        route = router(h, w_route[l], b_route[l])
        slot_tok, pos, block_expert, n_used = dispatch_plan(route[:, :TOP_K].astype(jnp.int32), n_blocks)
        xs = gather_rows(h, slot_tok)
        yb = expert_mlp(xs, block_expert, n_used, w_eg, w_eu, w_ed, l)
        x, xb = layer_out(h, hb, p_all[l], route, pos, yb, w_pg_b[l], w_pp_b[l], ln2_g[l], ln2_b[l])

        kv = lambda rows, c, b, s: proj[rows, c * W_ATTN:(c + 1) * W_ATTN].reshape(b, s, N_HEADS, HEAD_DIM)
        G, P = N_SSM_GROUPS, SSM_STATE
        for lst, val in zip(outs, (kv(slice(0, TP), 1, B, L), kv(slice(0, TP), 2, B, L),
                                   hr_p[:B].reshape(B, G, P), hi_p[:B].reshape(B, G, P),
                                   kv(slice(TP, T), 1, SB, SL), kv(slice(TP, T), 2, SB, SL),
                                   hr_s.reshape(SB, G, P), hi_s.reshape(SB, G, P))):
            lst.append(val)

    return (x[:TP].reshape(B, L, D), x[TP:].reshape(SB, SL, D)) + tuple(jnp.stack(o) for o in outs)
```

```python
import functools
import math

import jax
import jax.numpy as jnp
from jax import lax
from jax.experimental import pallas as pl
from jax.experimental.pallas import tpu as pltpu

F32 = jnp.float32
BF16 = jnp.bfloat16

D_MODEL = 2048
DEPTH = 4
W_ATTN = 1024
W_SSM = 1024
HEAD_DIM = 128
N_HEADS = 8
SSM_GROUP = 16
N_SSM_GROUPS = 64
SSM_STATE = 64
N_EXPERT_GROUPS = 4
EXPERTS_PER_GROUP = 8
N_EXPERTS = 32
TOP_K = 2
D_EXPERT = 512
PLE_DIM = 256
ALPHA = (2 * DEPTH) ** 0.25
LN_EPS = 1e-5

LANES = 128
SUBLANES = 8
VMEM_LIMIT = 56 * 1024 * 1024
SSM_LANES = N_SSM_GROUPS * SSM_STATE
SSM_PIECES = 8
ATT_BLK = 128
ATT_TQ = 512
ATT_UNROLL = 4
MOE_ROWS = 256


def _cparams(*sem):
    return pltpu.CompilerParams(dimension_semantics=sem, vmem_limit_bytes=VMEM_LIMIT)


def _dot(a, b):
    return jnp.dot(a, b, preferred_element_type=F32)


def _dot_nt(a, b):
    return lax.dot_general(a, b, (((1,), (1,)), ((), ())), preferred_element_type=F32)


def _split_bf16(x):
    hi = x.astype(BF16)
    lo = (x - hi.astype(F32)).astype(BF16)
    return hi, lo


def _layer_norm(x, g, b):
    mu = jnp.mean(x, -1, keepdims=True)
    xc = x - mu
    var = jnp.mean(xc * xc, -1, keepdims=True)
    return xc * lax.rsqrt(var + LN_EPS) * g + b


def _rms_norm(x, g):
    return x * lax.rsqrt(jnp.mean(x * x, -1, keepdims=True) + LN_EPS) * g


def _sigmoid(x):
    return 1.0 / (1.0 + jnp.exp(-x))


def _ln_in_kernel(x_ref, g_ref, b_ref, o_ref, ob_ref):
    y = _layer_norm(x_ref[...], g_ref[...], b_ref[...])
    o_ref[...] = y
    ob_ref[...] = y.astype(BF16)


def ln_in(x, g, b, tm=256):
    T, D = x.shape
    row = pl.BlockSpec((tm, D), lambda i: (i, 0))
    vec = pl.BlockSpec((1, D), lambda i: (0, 0))
    return pl.pallas_call(
        _ln_in_kernel, grid=(T // tm,), in_specs=[row, vec, vec], out_specs=[row, row],
        out_shape=[jax.ShapeDtypeStruct((T, D), F32), jax.ShapeDtypeStruct((T, D), BF16)],
        compiler_params=_cparams("parallel"), name="ln_in")(x, g.reshape(1, D), b.reshape(1, D))


def _matmul_kernel(x_ref, w_ref, o_ref):
    o_ref[...] = _dot(x_ref[...], w_ref[...])


def matmul(x, w, tm=1024, tn=512):
    M, K = x.shape
    N = w.shape[1]
    return pl.pallas_call(
        _matmul_kernel, grid=(M // tm, N // tn),
        in_specs=[pl.BlockSpec((tm, K), lambda i, j: (i, 0)), pl.BlockSpec((K, tn), lambda i, j: (0, j))],
        out_specs=pl.BlockSpec((tm, tn), lambda i, j: (i, j)),
        out_shape=jax.ShapeDtypeStruct((M, N), F32),
        compiler_params=_cparams("parallel", "arbitrary"), name="w_in_proj")(x, w)


def _suffix_sum_matrix(n, width):
    r = lax.broadcasted_iota(jnp.int32, (2 * n, 2 * width), 0)
    c = lax.broadcasted_iota(jnp.int32, (2 * n, 2 * width), 1)
    j = jnp.where(r >= n, r - n, r)
    later = jnp.logical_or(c < width, jnp.logical_and(c < width + n, j > c - width))
    return jnp.where(later, 1.0, 0.0).astype(BF16)


def _sb_chunk(qs, kc, vc, run, sums, diag):
    rows, nk = qs.shape[0], kc.shape[0]
    bw = min(nk, ATT_BLK)
    nb = nk // bw
    z = _dot_nt(qs, kc)
    sp = jnp.maximum(z, 0.0) + jnp.log(1.0 + jnp.exp(-jnp.abs(z)))
    blk = lambda a, b: a[:, b * bw:(b + 1) * bw]
    sps = [blk(sp, b) for b in range(nb)]
    if diag:
        visible = (lax.broadcasted_iota(jnp.int32, (rows, bw), 1) < lax.broadcasted_iota(jnp.int32, (rows, bw), 0))
        sps[-1] = jnp.where(visible, sps[-1], 0.0)
    sums_of = [_dot(jnp.concatenate(_split_bf16(s), axis=-1), sums) for s in sps]
    ws = [None] * nb
    for b in reversed(range(nb)):
        later = sums_of[b][:, LANES:LANES + bw]
        w = jnp.exp(blk(z, b) - sps[b] - later - run[:, :bw])
        if diag and b == nb - 1:
            w = jnp.where(visible, w, 0.0)
        ws[b] = w.astype(BF16)
        run = run + sums_of[b][:, :LANES]
    return _dot(jnp.concatenate(ws, axis=-1), vc), run


def _attn_prompt_kernel(q_ref, k_ref, v_ref, o_ref, kb_ref, vb_ref, acc_ref, run_ref):
    qi = pl.program_id(2)
    nsub = ATT_TQ // ATT_BLK

    @pl.when(qi == 0)
    def _():
        kb_ref[...] = k_ref[...].astype(BF16)
        vb_ref[...] = v_ref[...].astype(BF16)

    sums = _suffix_sum_matrix(ATT_BLK, LANES)
    qs = (q_ref[...] * (HEAD_DIM ** -0.5)).astype(BF16)
    tile0 = pl.multiple_of(qi * ATT_TQ, ATT_TQ)

    for s in range(nsub):
        rows = slice(s * ATT_BLK, (s + 1) * ATT_BLK)
        nk = (s + 1) * ATT_BLK
        pv, run = _sb_chunk(qs[rows], kb_ref[pl.ds(tile0, nk), :], vb_ref[pl.ds(tile0, nk), :],
                            jnp.zeros((ATT_BLK, LANES), F32), sums, True)
        acc_ref[rows, :] = pv
        run_ref[rows, :] = run

    def body(it, c):
        start = pl.multiple_of((qi - 1 - it) * ATT_TQ, ATT_TQ)
        for s in range(nsub):
            rows = slice(s * ATT_BLK, (s + 1) * ATT_BLK)
            pv, run = _sb_chunk(qs[rows], kb_ref[pl.ds(start, ATT_TQ), :], vb_ref[pl.ds(start, ATT_TQ), :],
                                run_ref[rows, :], sums, False)
            acc_ref[rows, :] += pv
            run_ref[rows, :] = run
        return c

    lax.fori_loop(0, qi, body, 0)
    o_ref[...] = acc_ref[...]


def attn_prompt(proj, batch, seq):
    nq = seq // ATT_TQ
    q_spec = pl.BlockSpec((ATT_TQ, HEAD_DIM), lambda b, h, i: (b * nq + i, h))
    k_spec = pl.BlockSpec((seq, HEAD_DIM), lambda b, h, i: (b, N_HEADS + h))
    v_spec = pl.BlockSpec((seq, HEAD_DIM), lambda b, h, i: (b, 2 * N_HEADS + h))
    return pl.pallas_call(
        _attn_prompt_kernel, grid=(batch, N_HEADS, nq), in_specs=[q_spec, k_spec, v_spec],
        out_specs=q_spec, out_shape=jax.ShapeDtypeStruct((batch * seq, W_ATTN), F32),
        scratch_shapes=[pltpu.VMEM((seq, HEAD_DIM), BF16), pltpu.VMEM((seq, HEAD_DIM), BF16),
                        pltpu.VMEM((ATT_TQ, HEAD_DIM), F32), pltpu.VMEM((ATT_TQ, LANES), F32)],
        compiler_params=_cparams("parallel", "parallel", "arbitrary"), name="attn_prompt")(proj, proj, proj)


def _attn_sample_kernel(q_ref, k_ref, v_ref, ck_ref, cv_ref, o_ref, qs_ref, kn_ref, vn_ref, oh_ref, *, past, tq):
    for h in range(N_HEADS):
        cols = slice(h * HEAD_DIM, (h + 1) * HEAD_DIM)
        qs_ref[h] = (q_ref[:, cols] * (HEAD_DIM ** -0.5)).astype(BF16)
        kn_ref[h] = k_ref[:, cols].astype(BF16)
        vn_ref[h] = v_ref[:, cols].astype(BF16)
    sums_new = _suffix_sum_matrix(tq, LANES)
    sums = _suffix_sum_matrix(ATT_BLK, LANES)

    def head(h, c):
        qs = qs_ref[h]
        acc, run = _sb_chunk(qs, kn_ref[h], vn_ref[h], jnp.zeros((tq, LANES), F32), sums_new, True)
        cached = pl.ds(h, past, stride=N_HEADS)
        pv, _ = _sb_chunk(qs, ck_ref[cached, :].astype(BF16), cv_ref[cached, :].astype(BF16), run, sums, False)
        oh_ref[h] = acc + pv
        return c

    lax.fori_loop(0, N_HEADS, head, 0)
    for h in range(N_HEADS):
        o_ref[:, h * HEAD_DIM:(h + 1) * HEAD_DIM] = oh_ref[h]


def attn_sample(proj, cache_k, cache_v, layer, row0, batch, tq):
    past = cache_k.shape[1] // N_HEADS
    rb0 = row0 // tq
    new = lambda c: pl.BlockSpec((tq, W_ATTN), lambda b: (rb0 + b, c))
    cache = pl.BlockSpec((None, past * N_HEADS, HEAD_DIM), lambda b: (layer * batch + b, 0, 0))
    per_head = lambda dt: pltpu.VMEM((N_HEADS, tq, HEAD_DIM), dt)
    return pl.pallas_call(
        functools.partial(_attn_sample_kernel, past=past, tq=tq), grid=(batch,),
        in_specs=[new(0), new(1), new(2), cache, cache],
        out_specs=pl.BlockSpec((tq, W_ATTN), lambda b: (b, 0)),
        out_shape=jax.ShapeDtypeStruct((batch * tq, W_ATTN), F32),
        scratch_shapes=[per_head(BF16), per_head(BF16), per_head(BF16), per_head(F32)],
        compiler_params=_cparams("parallel"), name="attn_sample")(proj, proj, proj, cache_k, cache_v)


def _ssm_param_kernel(lr_ref, li_ref, ls_ref, br_ref, bi_ref, ar_ref, ai_ref, bbr_ref, bbi_ref):
    lam_re = jnp.minimum(lr_ref[0], -1e-4)
    lam_im = li_ref[0]
    step = jnp.exp(ls_ref[0])
    mag = jnp.exp(lam_re * step)
    a_re = mag * jnp.cos(lam_im * step)
    a_im = mag * jnp.sin(lam_im * step)
    ar_ref[0] = a_re
    ai_ref[0] = a_im
    den = lam_re * lam_re + lam_im * lam_im
    nr = a_re - 1.0
    c_re = (nr * lam_re + a_im * lam_im) / den
    c_im = (a_im * lam_re - nr * lam_im) / den
    for c in range(SSM_GROUP):
        b_re = br_ref[0, c]
        b_im = bi_ref[0, c]
        bbr_ref[0, c] = c_re * b_re - c_im * b_im
        bbi_ref[0, c] = c_re * b_im + c_im * b_re


def ssm_params(lam_re, lam_im, log_step, b_re, b_im):
    G, P, C = N_SSM_GROUPS, SSM_STATE, SSM_GROUP
    gp = pl.BlockSpec((1, G, P), lambda l: (l, 0, 0))
    cgp = pl.BlockSpec((1, C, G, P), lambda l: (l, 0, 0, 0))
    return pl.pallas_call(
        _ssm_param_kernel, grid=(DEPTH,),
        in_specs=[gp, gp, pl.BlockSpec((1, G, 1), lambda l: (l, 0, 0)), cgp, cgp],
        out_specs=[gp, gp, cgp, cgp],
        out_shape=[jax.ShapeDtypeStruct((DEPTH, G, P), F32)] * 2 + [jax.ShapeDtypeStruct((DEPTH, C, G, P), F32)] * 2,
        compiler_params=_cparams("parallel"), name="ssm_params")(
            lam_re, lam_im, log_step.reshape(DEPTH, G, 1),
            b_re.transpose(0, 3, 1, 2), b_im.transpose(0, 3, 1, 2))


def _gelu_tanh(y):
    return 0.5 * y * (1.0 + jnp.tanh(math.sqrt(2.0 / math.pi) * (y + 0.044715 * (y * y * y))))


def _ssm_kernel(u_ref, h0r_ref, h0i_ref, ar_ref, ai_ref, wb_ref, wcr_ref, wci_ref, d_ref,
                z_ref, sr_ref, si_ref, hr_ref, hi_ref, wbh_ref, wbl_ref, *, rows, steps, lane_blk):
    t = pl.program_id(0)
    piece_in = SSM_LANES // SSM_PIECES
    piece_ch = W_SSM // SSM_PIECES

    @pl.when(t == 0)
    def _():
        sr_ref[...] = h0r_ref[...]
        si_ref[...] = h0i_ref[...]
        w = wb_ref[...]
        hi = w.astype(BF16)
        wbh_ref[...] = hi
        wbl_ref[...] = (w - hi.astype(F32)).astype(BF16)

    for k in range(SSM_PIECES):
        uk = u_ref[:, k * piece_ch:(k + 1) * piece_ch]
        uh, ul = _split_bf16(uk)
        bu = _dot(uh, wbh_ref[k]) + _dot(uh, wbl_ref[k]) + _dot(ul, wbh_ref[k])
        hr_ref[:, k * piece_in:(k + 1) * piece_in] = bu[:, :piece_in]
        hi_ref[:, k * piece_in:(k + 1) * piece_in] = bu[:, piece_in:]

    for lb in range(SSM_LANES // lane_blk):
        ln = slice(lb * lane_blk, (lb + 1) * lane_blk)
        a_re = jnp.broadcast_to(ar_ref[:, ln], (rows, lane_blk))
        a_im = jnp.broadcast_to(ai_ref[:, ln], (rows, lane_blk))

        def step(j, carry, ln=ln, a_re=a_re, a_im=a_im):
            h_re, h_im = carry
            r0 = pl.multiple_of(j * rows, rows)
            n_re = a_re * h_re - a_im * h_im + hr_ref[pl.ds(r0, rows), ln]
            n_im = a_re * h_im + a_im * h_re + hi_ref[pl.ds(r0, rows), ln]
            hr_ref[pl.ds(r0, rows), ln] = n_re
            hi_ref[pl.ds(r0, rows), ln] = n_im
            return n_re, n_im

        h_re, h_im = lax.fori_loop(0, steps, step, (sr_ref[:, ln], si_ref[:, ln]), unroll=4)
        sr_ref[:, ln] = h_re
        si_ref[:, ln] = h_im

    for k in range(SSM_PIECES):
        ln = slice(k * piece_in, (k + 1) * piece_in)
        ch = slice(k * piece_ch, (k + 1) * piece_ch)
        y = (_dot(hr_ref[:, ln].astype(BF16), wcr_ref[k].astype(BF16))
             - _dot(hi_ref[:, ln].astype(BF16), wci_ref[k].astype(BF16)))
        y = y + d_ref[:, ch] * u_ref[:, ch]
        z_ref[:, ch] = _gelu_tanh(y)


def ssm_scan(u_tm, h0_re, h0_im, a_re, a_im, wb, wc_re, wc_im, d, rows, steps):
    n = u_tm.shape[0] // (rows * steps)
    tile = rows * steps
    lane_blk = 4 * SUBLANES * LANES // rows
    full = lambda *s: pl.BlockSpec(s, lambda t: (0,) * len(s))
    return pl.pallas_call(
        functools.partial(_ssm_kernel, rows=rows, steps=steps, lane_blk=lane_blk), grid=(n,),
        in_specs=[pl.BlockSpec((tile, W_SSM), lambda t: (t, 0)),
                  full(rows, SSM_LANES), full(rows, SSM_LANES), full(1, SSM_LANES), full(1, SSM_LANES),
                  full(*wb.shape), full(*wc_re.shape), full(*wc_im.shape), full(1, W_SSM)],
        out_specs=[pl.BlockSpec((tile, W_SSM), lambda t: (t, 0)), full(rows, SSM_LANES), full(rows, SSM_LANES)],
        out_shape=[jax.ShapeDtypeStruct(u_tm.shape, F32)] + [jax.ShapeDtypeStruct((rows, SSM_LANES), F32)] * 2,
        scratch_shapes=[pltpu.VMEM((tile, SSM_LANES), F32), pltpu.VMEM((tile, SSM_LANES), F32),
                        pltpu.VMEM(wb.shape, BF16), pltpu.VMEM(wb.shape, BF16)],
        compiler_params=_cparams("arbitrary"), name="ssm_scan")(
            u_tm, h0_re, h0_im, a_re, a_im, wb, wc_re, wc_im, d)


def _block_diag(w):
    k, g, a, b = w.shape
    eye = jnp.eye(g, dtype=w.dtype)
    return (w[:, :, :, None, :] * eye[None, :, None, :, None]).reshape(k, g * a, g * b)


def ssm_weights(bb_re, bb_im, c_re, c_im):
    gpp = N_SSM_GROUPS // SSM_PIECES
    to_in = lambda b: b.transpose(1, 0, 2).reshape(SSM_PIECES, gpp, SSM_GROUP, SSM_STATE)
    wb = jnp.concatenate([_block_diag(to_in(bb_re)), _block_diag(to_in(bb_im))], axis=-1)
    to_out = lambda c: c.transpose(0, 2, 1).reshape(SSM_PIECES, gpp, SSM_STATE, SSM_GROUP)
    return wb, _block_diag(to_out(c_re)), _block_diag(to_out(c_im))


def _glu_kernel(z_ref, w_ref, b_ref, o_ref):
    z = z_ref[...]
    o_ref[...] = z * _sigmoid(_dot(z.astype(BF16), w_ref[...]) + b_ref[...])


def glu(z, w, b, tm=1024):
    M, N = z.shape
    row = pl.BlockSpec((tm, N), lambda i: (i, 0))
    return pl.pallas_call(
        _glu_kernel, grid=(M // tm,),
        in_specs=[row, pl.BlockSpec((N, N), lambda i: (0, 0)), pl.BlockSpec((1, N), lambda i: (0, 0))],
        out_specs=row, out_shape=jax.ShapeDtypeStruct((M, N), F32),
        compiler_params=_cparams("parallel"), name="ssm_glu")(z, w, b.reshape(1, N))


def _mix_out_kernel(a_ref, s_ref, x_ref, ga_ref, gs_ref, w_ref, g_ref, b_ref, h_ref, hb_ref):
    a = _rms_norm(a_ref[...], ga_ref[...]).astype(BF16)
    s = _rms_norm(s_ref[...], gs_ref[...]).astype(BF16)
    y = _dot(a, w_ref[:W_ATTN, :]) + _dot(s, w_ref[W_ATTN:, :])
    h = _layer_norm(ALPHA * x_ref[...] + y, g_ref[...], b_ref[...])
    h_ref[...] = h
    hb_ref[...] = h.astype(BF16)


def mix_out(attn, ssm, x, g_attn, g_ssm, w_out, ln_g, ln_b, tm=256):
    T, D = x.shape
    half = pl.BlockSpec((tm, W_ATTN), lambda i: (i, 0))
    row = pl.BlockSpec((tm, D), lambda i: (i, 0))
    vec = lambda n: pl.BlockSpec((1, n), lambda i: (0, 0))
    return pl.pallas_call(
        _mix_out_kernel, grid=(T // tm,),
        in_specs=[half, half, row, vec(W_ATTN), vec(W_SSM), pl.BlockSpec((D, D), lambda i: (0, 0)), vec(D), vec(D)],
        out_specs=[row, row],
        out_shape=[jax.ShapeDtypeStruct((T, D), F32), jax.ShapeDtypeStruct((T, D), BF16)],
        compiler_params=_cparams("parallel"), name="mix_out_ln1")(
            attn, ssm, x, g_attn.reshape(1, -1), g_ssm.reshape(1, -1), w_out, ln_g.reshape(1, D), ln_b.reshape(1, D))


def _router_kernel(h_ref, w_ref, b_ref, o_ref):
    tm = h_ref.shape[0]
    hh, hl = _split_bf16(h_ref[...])
    wh, wl = _split_bf16(w_ref[...])
    logits = _dot(hh, wh) + _dot(hh, wl) + _dot(hl, wh) + b_ref[...]
    lane = lax.broadcasted_iota(jnp.int32, (tm, LANES), 1).astype(F32)
    neg = -jnp.inf
    big = float(LANES)

    def first_max(v):
        m = jnp.max(v, -1, keepdims=True)
        return m, jnp.min(jnp.where(v == m, lane, big), -1, keepdims=True)

    is_group = lane < N_EXPERT_GROUPS
    g_max, g_idx = first_max(jnp.where(is_group, logits, neg))
    pg_top = 1.0 / jnp.sum(jnp.where(is_group, jnp.exp(logits - g_max), 0.0), -1, keepdims=True)
    lo = N_EXPERT_GROUPS + EXPERTS_PER_GROUP * g_idx
    le = jnp.where(lane >= lo, jnp.where(lane < lo + EXPERTS_PER_GROUP, logits, neg), neg)
    m1, i1 = first_max(le)
    m2, i2 = first_max(jnp.where(lane == i1, neg, le))
    e2 = jnp.exp(m2 - m1)
    den = 1.0 + e2
    o_ref[...] = jnp.where(lane == 0, i1 - N_EXPERT_GROUPS,
                 jnp.where(lane == 1, i2 - N_EXPERT_GROUPS,
                 jnp.where(lane == 2, pg_top / den,
                 jnp.where(lane == 3, pg_top * e2 / den, 0.0))))


def router(h, w_cat, b_cat, tm=512):
    T, D = h.shape
    return pl.pallas_call(
        _router_kernel, grid=(T // tm,),
        in_specs=[pl.BlockSpec((tm, D), lambda i: (i, 0)), pl.BlockSpec((D, LANES), lambda i: (0, 0)),
                  pl.BlockSpec((1, LANES), lambda i: (0, 0))],
        out_specs=pl.BlockSpec((tm, LANES), lambda i: (i, 0)),
        out_shape=jax.ShapeDtypeStruct((T, LANES), F32),
        compiler_params=_cparams("parallel"), name="router")(h, w_cat, b_cat)


def dispatch_plan(expert_ids, n_blocks):
    flat = expert_ids.reshape(-1)
    A = flat.shape[0]
    onehot = (flat[:, None] == jnp.arange(N_EXPERTS, dtype=jnp.int32)[None, :]).astype(jnp.int32)
    rank = jnp.sum((jnp.cumsum(onehot, axis=0) - 1) * onehot, axis=1)
    counts = jnp.sum(onehot, axis=0)
    blocks = (counts + MOE_ROWS - 1) // MOE_ROWS
    blk_end = jnp.cumsum(blocks)
    pad_start = (blk_end - blocks) * MOE_ROWS
    pos = pad_start[flat] + rank
    slot_tok = jnp.zeros((n_blocks * MOE_ROWS,), jnp.int32).at[pos].set(jnp.arange(A, dtype=jnp.int32) // TOP_K)
    n_used = blk_end[-1]
    blk = jnp.minimum(jnp.arange(n_blocks, dtype=jnp.int32), n_used - 1)
    block_expert = jnp.minimum(jnp.searchsorted(blk_end, blk, side='right'), N_EXPERTS - 1).astype(jnp.int32)
    return slot_tok, pos.astype(jnp.int32), block_expert, n_used.reshape(1).astype(jnp.int32)


def _row_copy(src_ref, dst_ref, src_row, dst_row, sem):
    return pltpu.make_async_copy(src_ref.at[pl.ds(src_row, 1)], dst_ref.at[pl.ds(dst_row, 1)], sem)


def _gather_rows_kernel(tok_ref, src_ref, o_ref, sem):
    rows = o_ref.shape[0]
    base = pl.program_id(0) * rows

    def start(r, c):
        _row_copy(src_ref, o_ref, tok_ref[base + r], r, sem).start()
        return c

    def wait(r, c):
        _row_copy(src_ref, o_ref, 0, r, sem).wait()
        return c

    lax.fori_loop(0, rows, start, 0, unroll=8)
    lax.fori_loop(0, rows, wait, 0, unroll=8)


def gather_rows(src, slot_tok, rows=MOE_ROWS):
    n = slot_tok.shape[0]
    D = src.shape[1]
    return pl.pallas_call(
        _gather_rows_kernel,
        grid_spec=pltpu.PrefetchScalarGridSpec(
            num_scalar_prefetch=1, grid=(n // rows,),
            in_specs=[pl.BlockSpec(memory_space=pl.ANY)],
            out_specs=pl.BlockSpec((rows, D), lambda i, tok: (i, 0)),
            scratch_shapes=[pltpu.SemaphoreType.DMA(())]),
        out_shape=jax.ShapeDtypeStruct((n, D), src.dtype),
        compiler_params=_cparams("arbitrary"), name="moe_gather")(slot_tok, src)


def _expert_kernel(be_ref, nu_ref, x_ref, wg_ref, wu_ref, wd_ref, o_ref, wgb_ref, wub_ref, wdb_ref):
    i = pl.program_id(0)
    changed = jnp.logical_or(i == 0, be_ref[i] != be_ref[jnp.maximum(i - 1, 0)])

    @pl.when(changed)
    def _():
        wgb_ref[...] = wg_ref[0].astype(BF16)
        wub_ref[...] = wu_ref[0].astype(BF16)
        wdb_ref[...] = wd_ref[0].astype(BF16)

    @pl.when(i < nu_ref[0])
    def _():
        x = x_ref[...].astype(BF16)
        g = _dot(x, wgb_ref[...])
        u = _dot(x, wub_ref[...])
        hid = (g * _sigmoid(g)) * u
        o_ref[...] = _dot(hid.astype(BF16), wdb_ref[...])

    @pl.when(i >= nu_ref[0])
    def _():
        o_ref[...] = jnp.zeros_like(o_ref)


def expert_mlp(xs, block_expert, n_used, w_gate, w_up, w_down, layer):
    n, D = xs.shape
    nb = n // MOE_ROWS
    return pl.pallas_call(
        _expert_kernel,
        grid_spec=pltpu.PrefetchScalarGridSpec(
            num_scalar_prefetch=2, grid=(nb,),
            in_specs=[pl.BlockSpec((MOE_ROWS, D), lambda i, be, nu: (jnp.minimum(i, nu[0] - 1), 0)),
                      pl.BlockSpec((1, D, D_EXPERT), lambda i, be, nu: (layer * N_EXPERTS + be[i], 0, 0)),
                      pl.BlockSpec((1, D, D_EXPERT), lambda i, be, nu: (layer * N_EXPERTS + be[i], 0, 0)),
                      pl.BlockSpec((1, D_EXPERT, D), lambda i, be, nu: (layer * N_EXPERTS + be[i], 0, 0))],
            out_specs=pl.BlockSpec((MOE_ROWS, D), lambda i, be, nu: (i, 0)),
            scratch_shapes=[pltpu.VMEM((D, D_EXPERT), BF16), pltpu.VMEM((D, D_EXPERT), BF16),
                            pltpu.VMEM((D_EXPERT, D), BF16)]),
        out_shape=jax.ShapeDtypeStruct((n, D), F32),
        compiler_params=_cparams("arbitrary"), name="moe_experts")(block_expert, n_used, xs, w_gate, w_up, w_down)


def _layer_out_kernel(pos_ref, h_ref, hb_ref, p_ref, r_ref, yb_ref, wg_ref, wp_ref, g_ref, b_ref,
                      o_ref, ob_ref, buf_ref, sem):
    tm = h_ref.shape[0]
    base = pl.program_id(0) * tm

    def start(r, c):
        for k in range(TOP_K):
            _row_copy(yb_ref, buf_ref.at[k], pos_ref[(base + r) * TOP_K + k], r, sem).start()
        return c

    def wait(r, c):
        for k in range(TOP_K):
            _row_copy(yb_ref, buf_ref.at[k], 0, r, sem).wait()
        return c

    lax.fori_loop(0, tm, start, 0, unroll=8)
    ple =_sigmoid(_dot(hb_ref[...], wg_ref[...])) * _dot(p_ref[...].astype(BF16), wp_ref[...])
    lax.fori_loop(0, tm, wait, 0, unroll=8)
    route = r_ref[...]
    moe = route[:, 2:3] * buf_ref[0] + route[:, 3:4] * buf_ref[1]
    x = _layer_norm(ALPHA * h_ref[...] + moe + ple, g_ref[...], b_ref[...])
    o_ref[...] = x
    ob_ref[...] = x.astype(BF16)


def layer_out(h, hb, p, route, pos, yb, w_ple_gate, w_ple_proj, ln_g, ln_b, tm=256):
    T, D = h.shape
    row = lambda n: pl.BlockSpec((tm, n), lambda i, pos: (i, 0))
    full = lambda a, b: pl.BlockSpec((a, b), lambda i, pos: (0, 0))
    return pl.pallas_call(
        _layer_out_kernel,
        grid_spec=pltpu.PrefetchScalarGridSpec(
            num_scalar_prefetch=1, grid=(T // tm,),
            in_specs=[row(D), row(D), row(PLE_DIM), row(LANES), pl.BlockSpec(memory_space=pl.ANY),
                      full(D, D), full(PLE_DIM, D), full(1, D), full(1, D)],
            out_specs=[row(D), row(D)],
            scratch_shapes=[pltpu.VMEM((TOP_K, tm, D), F32), pltpu.SemaphoreType.DMA(())]),
        out_shape=[jax.ShapeDtypeStruct((T, D), F32), jax.ShapeDtypeStruct((T, D), BF16)],
        compiler_params=_cparams("arbitrary"), name="moe_combine_ple_ln2")(
            pos, h, hb, p, route, yb, w_ple_gate, w_ple_proj, ln_g.reshape(1, D), ln_b.reshape(1, D))


def kernel(x_prompt, x_sample, p_prompt, p_sample, cache_k, cache_v, state_ssm_re, state_ssm_im, ln_in_g, ln_in_b, w_in, ssm_lam_re, ssm_lam_im, ssm_log_step, ssm_b_re, ssm_b_im, ssm_c_re, ssm_c_im, ssm_d, ssm_w_glu, ssm_b_glu, g_attn, g_ssm, w_out, ln1_g, ln1_b, w_router_group, b_router_group, w_router_expert, b_router_expert, w_exp_gate, w_exp_up, w_exp_down, w_ple_gate, w_ple_proj, ln2_g, ln2_b):
    B, L, D = x_prompt.shape
    SB, SL, _ = x_sample.shape
    depth = w_in.shape[0]
    past = cache_k.shape[2]
    TP, TS = B * L, SB * SL
    T = TP + TS
    n_blocks = (T * TOP_K) // MOE_ROWS + N_EXPERTS

    x, xb = ln_in(jnp.concatenate([x_prompt.reshape(TP, D), x_sample.reshape(TS, D)], axis=0), ln_in_g, ln_in_b)
    p_all = jnp.concatenate([p_prompt.reshape(depth, TP, PLE_DIM), p_sample.reshape(depth, TS, PLE_DIM)], axis=1)

    w_in_b = w_in.astype(BF16)
    w_out_b = w_out.astype(BF16)
    w_glu_b = ssm_w_glu.astype(BF16)
    w_pg_b = w_ple_gate.astype(BF16)
    w_pp_b = w_ple_proj.astype(BF16)
    w_route = jnp.pad(jnp.concatenate([w_router_group, w_router_expert], axis=-1),
                      ((0, 0), (0, 0), (0, LANES - N_EXPERT_GROUPS - N_EXPERTS)))
    b_route = jnp.pad(jnp.concatenate([b_router_group, b_router_expert], axis=-1),
                      ((0, 0), (0, LANES - N_EXPERT_GROUPS - N_EXPERTS))).reshape(depth, 1, LANES)
    a_re, a_im, bb_re, bb_im = ssm_params(ssm_lam_re, ssm_lam_im, ssm_log_step, ssm_b_re, ssm_b_im)

    cache_k_all = cache_k.reshape(depth * SB, past * N_HEADS, HEAD_DIM)
    cache_v_all = cache_v.reshape(depth * SB, past * N_HEADS, HEAD_DIM)
    w_eg = w_exp_gate.reshape(depth * N_EXPERTS, D, D_EXPERT)
    w_eu = w_exp_up.reshape(depth * N_EXPERTS, D, D_EXPERT)
    w_ed = w_exp_down.reshape(depth * N_EXPERTS, D_EXPERT, D)
    pad_rows = SUBLANES - B
    zeros_state = jnp.zeros((SUBLANES, SSM_LANES), F32)
    outs = [[] for _ in range(8)]
    for l in range(depth):
        proj = matmul(xb, w_in_b[l])
        attn_p = attn_prompt(proj, B, L)
        attn_s = attn_sample(proj, cache_k_all, cache_v_all, l, TP, SB, SL)

        wb, wc_re, wc_im = ssm_weights(bb_re[l], bb_im[l], ssm_c_re[l], ssm_c_im[l])
        lam_r = a_re[l].reshape(1, SSM_LANES)
        lam_i = a_im[l].reshape(1, SSM_LANES)
        d_row = ssm_d[l].reshape(1, W_SSM)
        u_p = proj[:TP, 3 * W_ATTN:].reshape(B, L, W_SSM).transpose(1, 0, 2)
        u_p = jnp.pad(u_p, ((0, 0), (0, pad_rows), (0, 0))).reshape(L * SUBLANES, W_SSM)
        z_p, hr_p, hi_p = ssm_scan(u_p, zeros_state, zeros_state, lam_r, lam_i, wb, wc_re, wc_im, d_row,
                                   rows=SUBLANES, steps=32)
        z_p = z_p.reshape(L, SUBLANES, W_SSM)[:, :B].transpose(1, 0, 2).reshape(TP, W_SSM)
        u_s = proj[TP:, 3 * W_ATTN:].reshape(SB, SL, W_SSM).transpose(1, 0, 2).reshape(TS, W_SSM)
        z_s, hr_s, hi_s = ssm_scan(u_s, state_ssm_re[l].reshape(SB, SSM_LANES), state_ssm_im[l].reshape(SB, SSM_LANES),
                                   lam_r, lam_i, wb, wc_re, wc_im, d_row, rows=SB, steps=8)
        z_s = z_s.reshape(SL, SB, W_SSM).transpose(1, 0, 2).reshape(TS, W_SSM)
        ssm = glu(jnp.concatenate([z_p, z_s], axis=0), w_glu_b[l], ssm_b_glu[l])

        h, hb = mix_out(jnp.concatenate([attn_p, attn_s], axis=0), ssm, x, g_attn[l], g_ssm[l], w_out_b[l],
                        ln1_g[l], ln1_b[l])
        route = router(h, w_route[l], b_route[l])
        slot_tok, pos, block_expert, n_used = dispatch_plan(route[:, :TOP_K].astype(jnp.int32), n_blocks)
        xs = gather_rows(h, slot_tok)
        yb = expert_mlp(xs, block_expert, n_used, w_eg, w_eu, w_ed, l)
        x, xb = layer_out(h, hb, p_all[l], route, pos, yb, w_pg_b[l], w_pp_b[l], ln2_g[l], ln2_b[l])

        kv = lambda rows, c, b, s: proj[rows, c * W_ATTN:(c + 1) * W_ATTN].reshape(b, s, N_HEADS, HEAD_DIM)
        G, P = N_SSM_GROUPS, SSM_STATE
        for lst, val in zip(outs, (kv(slice(0, TP), 1, B, L), kv(slice(0, TP), 2, B, L),
                                   hr_p[:B].reshape(B, G, P), hi_p[:B].reshape(B, G, P),
                                   kv(slice(TP, T), 1, SB, SL), kv(slice(TP, T), 2, SB, SL),
                                   hr_s.reshape(SB, G, P), hi_s.reshape(SB, G, P))):
            lst.append(val)

    return (x[:TP].reshape(B, L, D), x[TP:].reshape(SB, SL, D)) + tuple(jnp.stack(o) for o in outs)
```

```python
import functools
import math

import jax
import jax.numpy as jnp
from jax import lax
from jax.experimental import pallas as pl
from jax.experimental.pallas import tpu as pltpu

F32 = jnp.float32
BF16 = jnp.bfloat16

D_MODEL = 2048
DEPTH = 4
W_ATTN = 1024
W_SSM = 1024
HEAD_DIM = 128
N_HEADS = 8
SSM_GROUP = 16
N_SSM_GROUPS = 64
SSM_STATE = 64
N_EXPERT_GROUPS = 4
EXPERTS_PER_GROUP = 8
N_EXPERTS = 32
TOP_K = 2
D_EXPERT = 512
PLE_DIM = 256
ALPHA = (2 * DEPTH) ** 0.25
LN_EPS = 1e-5

LANES = 128
SUBLANES = 8
VMEM_LIMIT = 56 * 1024 * 1024
SSM_LANES = N_SSM_GROUPS * SSM_STATE
SSM_PIECES = 8
ATT_BLK = 128
ATT_TQ = 512
ATT_UNROLL = 4
MOE_ROWS = 256
TOK_TILE = D_MODEL // LANES


def _cparams(*sem):
    return pltpu.CompilerParams(dimension_semantics=sem, vmem_limit_bytes=VMEM_LIMIT)


def _dot(a, b):
    return jnp.dot(a, b, preferred_element_type=F32)


def _dot_nt(a, b):
    return lax.dot_general(a, b, (((1,), (1,)), ((), ())), preferred_element_type=F32)


def _split_bf16(x):
    hi = x.astype(BF16)
    lo = (x - hi.astype(F32)).astype(BF16)
    return hi, lo


def _layer_norm(x, g, b):
    mu = jnp.mean(x, -1, keepdims=True)
    xc = x - mu
    var = jnp.mean(xc * xc, -1, keepdims=True)
    return xc * lax.rsqrt(var + LN_EPS) * g + b


def _rms_norm(x, g):
    return x * lax.rsqrt(jnp.mean(x * x, -1, keepdims=True) + LN_EPS) * g


def _sigmoid(x):
    return 1.0 / (1.0 + jnp.exp(-x))


def _ln_in_kernel(x_ref, g_ref, b_ref, o_ref, ob_ref):
    y = _layer_norm(x_ref[...], g_ref[...], b_ref[...])
    o_ref[...] = y
    ob_ref[...] = y.astype(BF16)


def ln_in(x, g, b, tm=256):
    T, D = x.shape
    row = pl.BlockSpec((tm, D), lambda i: (i, 0))
    vec = pl.BlockSpec((1, D), lambda i: (0, 0))
    return pl.pallas_call(
        _ln_in_kernel, grid=(T // tm,), in_specs=[row, vec, vec], out_specs=[row, row],
        out_shape=[jax.ShapeDtypeStruct((T, D), F32), jax.ShapeDtypeStruct((T, D), BF16)],
        compiler_params=_cparams("parallel"), name="ln_in")(x, g.reshape(1, D), b.reshape(1, D))


def _matmul_kernel(x_ref, w_ref, o_ref):
    o_ref[...] = _dot(x_ref[...], w_ref[...])


def matmul(x, w, tm=1024, tn=512):
    M, K = x.shape
    N = w.shape[1]
    return pl.pallas_call(
        _matmul_kernel, grid=(M // tm, N // tn),
        in_specs=[pl.BlockSpec((tm, K), lambda i, j: (i, 0)), pl.BlockSpec((K, tn), lambda i, j: (0, j))],
        out_specs=pl.BlockSpec((tm, tn), lambda i, j: (i, j)),
        out_shape=jax.ShapeDtypeStruct((M, N), F32),
        compiler_params=_cparams("parallel", "arbitrary"), name="w_in_proj")(x, w)


def _suffix_sum_matrix(n, width):
    r = lax.broadcasted_iota(jnp.int32, (2 * n, 2 * width), 0)
    c = lax.broadcasted_iota(jnp.int32, (2 * n, 2 * width), 1)
    j = jnp.where(r >= n, r - n, r)
    later = jnp.logical_or(c < width, jnp.logical_and(c < width + n, j > c - width))
    return jnp.where(later, 1.0, 0.0).astype(BF16)


def _sb_chunk(qs, kc, vc, run, sums, diag):
    rows, nk = qs.shape[0], kc.shape[0]
    bw = min(nk, ATT_BLK)
    nb = nk // bw
    z = _dot_nt(qs, kc)
    sp = jnp.maximum(z, 0.0) + jnp.log(1.0 + jnp.exp(-jnp.abs(z)))
    blk = lambda a, b: a[:, b * bw:(b + 1) * bw]
    sps = [blk(sp, b) for b in range(nb)]
    if diag:
        visible = (lax.broadcasted_iota(jnp.int32, (rows, bw), 1) < lax.broadcasted_iota(jnp.int32, (rows, bw), 0))
        sps[-1] = jnp.where(visible, sps[-1], 0.0)
    sums_of = [_dot(jnp.concatenate(_split_bf16(s), axis=-1), sums) for s in sps]
    ws = [None] * nb
    for b in reversed(range(nb)):
        later = sums_of[b][:, LANES:LANES + bw]
        w = jnp.exp(blk(z, b) - sps[b] - later - run[:, :bw])
        if diag and b == nb - 1:
            w = jnp.where(visible, w, 0.0)
        ws[b] = w.astype(BF16)
        run = run + sums_of[b][:, :LANES]
    return _dot(jnp.concatenate(ws, axis=-1), vc), run


def _attn_prompt_kernel(q_ref, k_ref, v_ref, o_ref, kb_ref, vb_ref, acc_ref, run_ref):
    qi = pl.program_id(2)
    nsub = ATT_TQ // ATT_BLK

    @pl.when(qi == 0)
    def _():
        kb_ref[...] = k_ref[...].astype(BF16)
        vb_ref[...] = v_ref[...].astype(BF16)

    sums = _suffix_sum_matrix(ATT_BLK, LANES)
    qs = (q_ref[...] * (HEAD_DIM ** -0.5)).astype(BF16)
    tile0 = pl.multiple_of(qi * ATT_TQ, ATT_TQ)

    for s in range(nsub):
        rows = slice(s * ATT_BLK, (s + 1) * ATT_BLK)
        nk = (s + 1) * ATT_BLK
        pv, run = _sb_chunk(qs[rows], kb_ref[pl.ds(tile0, nk), :], vb_ref[pl.ds(tile0, nk), :],
                            jnp.zeros((ATT_BLK, LANES), F32), sums, True)
        acc_ref[rows, :] = pv
        run_ref[rows, :] = run

    def body(it, c):
        start = pl.multiple_of((qi - 1 - it) * ATT_TQ, ATT_TQ)
        for s in range(nsub):
            rows = slice(s * ATT_BLK, (s + 1) * ATT_BLK)
            pv, run = _sb_chunk(qs[rows], kb_ref[pl.ds(start, ATT_TQ), :], vb_ref[pl.ds(start, ATT_TQ), :],
                                run_ref[rows, :], sums, False)
            acc_ref[rows, :] += pv
            run_ref[rows, :] = run
        return c

    lax.fori_loop(0, qi, body, 0)
    o_ref[...] = acc_ref[...]


def attn_prompt(proj, batch, seq):
    nq = seq // ATT_TQ
    q_spec = pl.BlockSpec((ATT_TQ, HEAD_DIM), lambda b, h, i: (b * nq + i, h))
    k_spec = pl.BlockSpec((seq, HEAD_DIM), lambda b, h, i: (b, N_HEADS + h))
    v_spec = pl.BlockSpec((seq, HEAD_DIM), lambda b, h, i: (b, 2 * N_HEADS + h))
    return pl.pallas_call(
        _attn_prompt_kernel, grid=(batch, N_HEADS, nq), in_specs=[q_spec, k_spec, v_spec],
        out_specs=q_spec, out_shape=jax.ShapeDtypeStruct((batch * seq, W_ATTN), F32),
        scratch_shapes=[pltpu.VMEM((seq, HEAD_DIM), BF16), pltpu.VMEM((seq, HEAD_DIM), BF16),
                        pltpu.VMEM((ATT_TQ, HEAD_DIM), F32), pltpu.VMEM((ATT_TQ, LANES), F32)],
        compiler_params=_cparams("parallel", "parallel", "arbitrary"), name="attn_prompt")(proj, proj, proj)


def _attn_sample_kernel(q_ref, k_ref, v_ref, ck_ref, cv_ref, o_ref, qs_ref, kn_ref, vn_ref, oh_ref, *, past, tq):
    for h in range(N_HEADS):
        cols = slice(h * HEAD_DIM, (h + 1) * HEAD_DIM)
        qs_ref[h] = (q_ref[:, cols] * (HEAD_DIM ** -0.5)).astype(BF16)
        kn_ref[h] = k_ref[:, cols].astype(BF16)
        vn_ref[h] = v_ref[:, cols].astype(BF16)
    sums_new = _suffix_sum_matrix(tq, LANES)
    sums = _suffix_sum_matrix(ATT_BLK, LANES)

    def head(h, c):
        qs = qs_ref[h]
        acc, run = _sb_chunk(qs, kn_ref[h], vn_ref[h], jnp.zeros((tq, LANES), F32), sums_new, True)
        cached = pl.ds(h, past, stride=N_HEADS)
        pv, _ = _sb_chunk(qs, ck_ref[cached, :].astype(BF16), cv_ref[cached, :].astype(BF16), run, sums, False)
        oh_ref[h] = acc + pv
        return c

    lax.fori_loop(0, N_HEADS, head, 0)
    for h in range(N_HEADS):
        o_ref[:, h * HEAD_DIM:(h + 1) * HEAD_DIM] = oh_ref[h]


def attn_sample(proj, cache_k, cache_v, layer, row0, batch, tq):
    past = cache_k.shape[1] // N_HEADS
    rb0 = row0 // tq
    new = lambda c: pl.BlockSpec((tq, W_ATTN), lambda b: (rb0 + b, c))
    cache = pl.BlockSpec((None, past * N_HEADS, HEAD_DIM), lambda b: (layer * batch + b, 0, 0))
    per_head = lambda dt: pltpu.VMEM((N_HEADS, tq, HEAD_DIM), dt)
    return pl.pallas_call(
        functools.partial(_attn_sample_kernel, past=past, tq=tq), grid=(batch,),
        in_specs=[new(0), new(1), new(2), cache, cache],
        out_specs=pl.BlockSpec((tq, W_ATTN), lambda b: (b, 0)),
        out_shape=jax.ShapeDtypeStruct((batch * tq, W_ATTN), F32),
        scratch_shapes=[per_head(BF16), per_head(BF16), per_head(BF16), per_head(F32)],
        compiler_params=_cparams("parallel"), name="attn_sample")(proj, proj, proj, cache_k, cache_v)


def _ssm_param_kernel(lr_ref, li_ref, ls_ref, br_ref, bi_ref, ar_ref, ai_ref, bbr_ref, bbi_ref):
    lam_re = jnp.minimum(lr_ref[0], -1e-4)
    lam_im = li_ref[0]
    step = jnp.exp(ls_ref[0])
    mag = jnp.exp(lam_re * step)
    a_re = mag * jnp.cos(lam_im * step)
    a_im = mag * jnp.sin(lam_im * step)
    ar_ref[0] = a_re
    ai_ref[0] = a_im
    den = lam_re * lam_re + lam_im * lam_im
    nr = a_re - 1.0
    c_re = (nr * lam_re + a_im * lam_im) / den
    c_im = (a_im * lam_re - nr * lam_im) / den
    for c in range(SSM_GROUP):
        b_re = br_ref[0, c]
        b_im = bi_ref[0, c]
        bbr_ref[0, c] = c_re * b_re - c_im * b_im
        bbi_ref[0, c] = c_re * b_im + c_im * b_re


def ssm_params(lam_re, lam_im, log_step, b_re, b_im):
    G, P, C = N_SSM_GROUPS, SSM_STATE, SSM_GROUP
    gp = pl.BlockSpec((1, G, P), lambda l: (l, 0, 0))
    cgp = pl.BlockSpec((1, C, G, P), lambda l: (l, 0, 0, 0))
    return pl.pallas_call(
        _ssm_param_kernel, grid=(DEPTH,),
        in_specs=[gp, gp, pl.BlockSpec((1, G, 1), lambda l: (l, 0, 0)), cgp, cgp],
        out_specs=[gp, gp, cgp, cgp],
        out_shape=[jax.ShapeDtypeStruct((DEPTH, G, P), F32)] * 2 + [jax.ShapeDtypeStruct((DEPTH, C, G, P), F32)] * 2,
        compiler_params=_cparams("parallel"), name="ssm_params")(
            lam_re, lam_im, log_step.reshape(DEPTH, G, 1),
            b_re.transpose(0, 3, 1, 2), b_im.transpose(0, 3, 1, 2))


def _gelu_tanh(y):
    return 0.5 * y * (1.0 + jnp.tanh(math.sqrt(2.0 / math.pi) * (y + 0.044715 * (y * y * y))))


def _ssm_kernel(*refs, n_seq, rows, steps, seq_len, whole):
    n_io = 1 if whole else n_seq
    u_refs = refs[:n_io]
    (h0r_ref, h0i_ref, ar_ref, ai_ref, wb_ref, wcr_ref, wci_ref, d_ref, wg_ref, bg_ref) = refs[n_io:n_io + 10]
    o_ref, sr_ref, si_ref, us_ref, hr_ref, hi_ref, zs_ref = refs[n_io + 10:]
    t = pl.program_id(0)
    piece_in = SSM_LANES // SSM_PIECES
    piece_ch = W_SSM // SSM_PIECES
    lane_blk = 4 * SUBLANES * LANES // rows

    def seq_rows(b):
        return pl.ds(pl.multiple_of(b * seq_len + t * steps, steps), steps)

    @pl.when(t == 0)
    def _():
        sr_ref[...] = h0r_ref[...]
        si_ref[...] = h0i_ref[...]
        if n_seq < rows:
            us_ref[...] = jnp.zeros_like(us_ref)

    for b in range(n_seq):
        ub = u_refs[0][seq_rows(b), :] if whole else u_refs[b][...]
        for k in range(SSM_PIECES):
            us_ref[k, pl.ds(b, steps, stride=rows), :] = ub[:, k * piece_ch:(k + 1) * piece_ch]

    for k in range(SSM_PIECES):
        bu = _dot(us_ref[k].astype(BF16), wb_ref[k])
        hr_ref[:, k * piece_in:(k + 1) * piece_in] = bu[:, :piece_in]
        hi_ref[:, k * piece_in:(k + 1) * piece_in] = bu[:, piece_in:]

    for lb in range(SSM_LANES // lane_blk):
        ln = slice(lb * lane_blk, (lb + 1) * lane_blk)
        a_re = jnp.broadcast_to(ar_ref[:, ln], (rows, lane_blk))
        a_im = jnp.broadcast_to(ai_ref[:, ln], (rows, lane_blk))

        def step(j, carry, ln=ln, a_re=a_re, a_im=a_im):
            h_re, h_im = carry
            r0 = pl.multiple_of(j * rows, rows)
            n_re = a_re * h_re - a_im * h_im + hr_ref[pl.ds(r0, rows), ln]
            n_im = a_re * h_im + a_im * h_re + hi_ref[pl.ds(r0, rows), ln]
            hr_ref[pl.ds(r0, rows), ln] = n_re
            hi_ref[pl.ds(r0, rows), ln] = n_im
            return n_re, n_im

        h_re, h_im = lax.fori_loop(0, steps, step, (sr_ref[:, ln], si_ref[:, ln]), unroll=4)
        sr_ref[:, ln] = h_re
        si_ref[:, ln] = h_im

    zs = []
    for k in range(SSM_PIECES):
        ln = slice(k * piece_in, (k + 1) * piece_in)
        y = _dot(hr_ref[:, ln].astype(BF16), wcr_ref[k]) - _dot(hi_ref[:, ln].astype(BF16), wci_ref[k])
        zs.append(_gelu_tanh(y + d_ref[:, k * piece_ch:(k + 1) * piece_ch] * us_ref[k]))
    z = jnp.concatenate(zs, axis=-1)
    out = z * _sigmoid(_dot(z.astype(BF16), wg_ref[...]) + bg_ref[...])
    for k in range(SSM_PIECES):
        zs_ref[k] = out[:, k * piece_ch:(k + 1) * piece_ch]

    for b in range(n_seq):
        ob = jnp.concatenate([zs_ref[k, pl.ds(b, steps, stride=rows), :] for k in range(SSM_PIECES)], axis=-1)
        if whole:
            o_ref[seq_rows(b), :] = ob
        else:
            o_ref[b] = ob


def ssm_mixer(proj, row0, n_seq, seq_len, h0_re, h0_im, a_re, a_im, wb, wc_re, wc_im, d, w_glu, b_glu, steps, whole):
    rows = h0_re.shape[0]
    tile = rows * steps
    u_col = proj.shape[1] // W_SSM - 1
    full = lambda *s: pl.BlockSpec(s, lambda t: (0,) * len(s))
    if whole:
        u_specs = [pl.BlockSpec((n_seq * seq_len, W_SSM), lambda t: (row0 // (n_seq * seq_len), u_col))]
        o_spec = full(n_seq * seq_len, W_SSM)
        o_shape = jax.ShapeDtypeStruct((n_seq * seq_len, W_SSM), F32)
    else:
        u_specs = [pl.BlockSpec((steps, W_SSM), lambda t, b=b: (b * (seq_len // steps) + t, u_col)) for b in range(n_seq)]
        o_spec = pl.BlockSpec((n_seq, steps, W_SSM), lambda t: (0, t, 0))
        o_shape = jax.ShapeDtypeStruct((n_seq, seq_len, W_SSM), F32)
    state = full(rows, SSM_LANES)
    out, s_re, s_im = pl.pallas_call(
        functools.partial(_ssm_kernel, n_seq=n_seq, rows=rows, steps=steps, seq_len=seq_len, whole=whole),
        grid=(seq_len // steps,),
        in_specs=u_specs + [state, state, full(1, SSM_LANES), full(1, SSM_LANES), full(*wb.shape),
                            full(*wc_re.shape), full(*wc_im.shape), full(1, W_SSM), full(W_SSM, W_SSM), full(1, W_SSM)],
        out_specs=[o_spec, state, state],
        out_shape=[o_shape] + [jax.ShapeDtypeStruct((rows, SSM_LANES), F32)] * 2,
        scratch_shapes=[pltpu.VMEM((SSM_PIECES, tile, LANES), F32), pltpu.VMEM((tile, SSM_LANES), F32),
                        pltpu.VMEM((tile, SSM_LANES), F32), pltpu.VMEM((SSM_PIECES, tile, LANES), F32)],
        compiler_params=_cparams("arbitrary"), name="ssm_mixer")(
            *([proj] * len(u_specs)), h0_re, h0_im, a_re, a_im, wb, wc_re, wc_im, d, w_glu, b_glu.reshape(1, W_SSM))
    return out.reshape(n_seq, seq_len, W_SSM), s_re, s_im


def _block_diag(w):
    k, g, a, b = w.shape
    eye = jnp.eye(g, dtype=w.dtype)
    return (w[:, :, :, None, :] * eye[None, :, None, :, None]).reshape(k, g * a, g * b)


def ssm_weights(bb_re, bb_im, c_re, c_im):
    gpp = N_SSM_GROUPS // SSM_PIECES
    to_in = lambda b: b.transpose(1, 0, 2).reshape(SSM_PIECES, gpp, SSM_GROUP, SSM_STATE)
    wb = jnp.concatenate([_block_diag(to_in(bb_re)), _block_diag(to_in(bb_im))], axis=-1)
    to_out = lambda c: c.transpose(0, 2, 1).reshape(SSM_PIECES, gpp, SSM_STATE, SSM_GROUP)
    return wb.astype(BF16), _block_diag(to_out(c_re)).astype(BF16), _block_diag(to_out(c_im)).astype(BF16)


def _store_token_tiles(ref, x):
    for c in range(TOK_TILE):
        ref[pl.ds(c, x.shape[0], stride=TOK_TILE), :] = x[:, c * LANES:(c + 1) * LANES]


def _load_token_tiles(ref, rows):
    return jnp.concatenate([ref[pl.ds(c, rows, stride=TOK_TILE), :] for c in range(TOK_TILE)], axis=-1)


def _mix_out_kernel(a_ref, s_ref, x_ref, ga_ref, gs_ref, w_ref, g_ref, b_ref, h_ref, hb_ref, ht_ref):
    a = _rms_norm(a_ref[...], ga_ref[...]).astype(BF16)
    s = _rms_norm(s_ref[...], gs_ref[...]).astype(BF16)
    y = _dot(a, w_ref[:W_ATTN, :]) + _dot(s, w_ref[W_ATTN:, :])
    h = _layer_norm(ALPHA * x_ref[...] + y, g_ref[...], b_ref[...])
    h_ref[...] = h
    hb_ref[...] = h.astype(BF16)
    _store_token_tiles(ht_ref, h)


def mix_out(attn, ssm, x, g_attn, g_ssm, w_out, ln_g, ln_b, tm=256):
    T, D = x.shape
    half = pl.BlockSpec((tm, W_ATTN), lambda i: (i, 0))
    row = pl.BlockSpec((tm, D), lambda i: (i, 0))
    vec = lambda n: pl.BlockSpec((1, n), lambda i: (0, 0))
    return pl.pallas_call(
        _mix_out_kernel, grid=(T // tm,),
        in_specs=[half, half, row, vec(W_ATTN), vec(W_SSM), pl.BlockSpec((D, D), lambda i: (0, 0)), vec(D), vec(D)],
        out_specs=[row, row, pl.BlockSpec((tm * TOK_TILE, LANES), lambda i: (i, 0))],
        out_shape=[jax.ShapeDtypeStruct((T, D), F32), jax.ShapeDtypeStruct((T, D), BF16),
                   jax.ShapeDtypeStruct((T * TOK_TILE, LANES), F32)],
        compiler_params=_cparams("parallel"), name="mix_out_ln1")(
            attn, ssm, x, g_attn.reshape(1, -1), g_ssm.reshape(1, -1), w_out, ln_g.reshape(1, D), ln_b.reshape(1, D))


def _router_kernel(h_ref, w_ref, b_ref, o_ref):
    tm = h_ref.shape[0]
    hh, hl = _split_bf16(h_ref[...])
    wh, wl = _split_bf16(w_ref[...])
    logits = _dot(hh, wh) + _dot(hh, wl) + _dot(hl, wh) + b_ref[...]
    lane = lax.broadcasted_iota(jnp.int32, (tm, LANES), 1).astype(F32)
    neg = -jnp.inf
    big = float(LANES)

    def first_max(v):
        m = jnp.max(v, -1, keepdims=True)
        return m, jnp.min(jnp.where(v == m, lane, big), -1, keepdims=True)

    is_group = lane < N_EXPERT_GROUPS
    g_max, g_idx = first_max(jnp.where(is_group, logits, neg))
    pg_top = 1.0 / jnp.sum(jnp.where(is_group, jnp.exp(logits - g_max), 0.0), -1, keepdims=True)
    lo = N_EXPERT_GROUPS + EXPERTS_PER_GROUP * g_idx
    le = jnp.where(lane >= lo, jnp.where(lane < lo + EXPERTS_PER_GROUP, logits, neg), neg)
    m1, i1 = first_max(le)
    m2, i2 = first_max(jnp.where(lane == i1, neg, le))
    e2 = jnp.exp(m2 - m1)
    den = 1.0 + e2
    o_ref[...] = jnp.where(lane == 0, i1 - N_EXPERT_GROUPS,
                 jnp.where(lane == 1, i2 - N_EXPERT_GROUPS,
                 jnp.where(lane == 2, pg_top / den,
                 jnp.where(lane == 3, pg_top * e2 / den, 0.0))))


def router(h, w_cat, b_cat, tm=512):
    T, D = h.shape
    return pl.pallas_call(
        _router_kernel, grid=(T // tm,),
        in_specs=[pl.BlockSpec((tm, D), lambda i: (i, 0)), pl.BlockSpec((D, LANES), lambda i: (0, 0)),
                  pl.BlockSpec((1, LANES), lambda i: (0, 0))],
        out_specs=pl.BlockSpec((tm, LANES), lambda i: (i, 0)),
        out_shape=jax.ShapeDtypeStruct((T, LANES), F32),
        compiler_params=_cparams("parallel"), name="router")(h, w_cat, b_cat)


def dispatch_plan(expert_ids, n_blocks):
    flat = expert_ids.reshape(-1)
    A = flat.shape[0]
    onehot = (flat[:, None] == jnp.arange(N_EXPERTS, dtype=jnp.int32)[None, :]).astype(jnp.int32)
    rank = jnp.sum((jnp.cumsum(onehot, axis=0) - 1) * onehot, axis=1)
    counts = jnp.sum(onehot, axis=0)
    blocks = (counts + MOE_ROWS - 1) // MOE_ROWS
    blk_end = jnp.cumsum(blocks)
    pad_start = (blk_end - blocks) * MOE_ROWS
    pos = pad_start[flat] + rank
    slot_tok = jnp.zeros((n_blocks * MOE_ROWS,), jnp.int32).at[pos].set(jnp.arange(A, dtype=jnp.int32) // TOP_K)
    n_used = blk_end[-1]
    blk = jnp.minimum(jnp.arange(n_blocks, dtype=jnp.int32), n_used - 1)
    block_expert = jnp.minimum(jnp.searchsorted(blk_end, blk, side='right'), N_EXPERTS - 1).astype(jnp.int32)
    return slot_tok, pos.astype(jnp.int32), block_expert, n_used.reshape(1).astype(jnp.int32)


def _tile_copy(src_ref, dst_ref, src_tok, dst_tok, sem):
    return pltpu.make_async_copy(src_ref.at[pl.ds(src_tok * TOK_TILE, TOK_TILE)],
                                 dst_ref.at[pl.ds(dst_tok * TOK_TILE, TOK_TILE)], sem)


def _expert_kernel(tok_ref, be_ref, nu_ref, ht_ref, wg_ref, wu_ref, wd_ref, o_ref,
                   xbuf_ref, wgb_ref, wub_ref, wdb_ref, sem):
    i = pl.program_id(0)
    n_used = nu_ref[0]
    slot = lax.rem(i, 2)

    def start_gather(block, s):
        def body(r, c):
            _tile_copy(ht_ref, xbuf_ref.at[s], tok_ref[block * MOE_ROWS + r], r, sem.at[s]).start()
            return c
        lax.fori_loop(0, MOE_ROWS, body, 0, unroll=8)

    def wait_gather(s):
        def body(r, c):
            _tile_copy(ht_ref, xbuf_ref.at[s], 0, r, sem.at[s]).wait()
            return c
        lax.fori_loop(0, MOE_ROWS, body, 0, unroll=8)

    @pl.when(i == 0)
    def _():
        start_gather(0, 0)

    @pl.when(i + 1 < n_used)
    def _():
        start_gather(i + 1, 1 - slot)

    @pl.when(jnp.logical_or(i == 0, be_ref[i] != be_ref[jnp.maximum(i - 1, 0)]))
    def _():
        wgb_ref[...] = wg_ref[0].astype(BF16)
        wub_ref[...] = wu_ref[0].astype(BF16)
        wdb_ref[...] = wd_ref[0].astype(BF16)

    @pl.when(i < n_used)
    def _():
        wait_gather(slot)
        x = _load_token_tiles(xbuf_ref.at[slot], MOE_ROWS).astype(BF16)
        g = _dot(x, wgb_ref[...])
        u = _dot(x, wub_ref[...])
        hid = (g * _sigmoid(g)) * u
        _store_token_tiles(o_ref, _dot(hid.astype(BF16), wdb_ref[...]))

    @pl.when(i >= n_used)
    def _():
        o_ref[...] = jnp.zeros_like(o_ref)


def expert_mlp(h_tiles, slot_tok, block_expert, n_used, w_gate, w_up, w_down, layer):
    nb = block_expert.shape[0]
    blk_rows = MOE_ROWS * TOK_TILE
    D = D_MODEL
    weight = lambda a, b: pl.BlockSpec((1, a, b), lambda i, tok, be, nu: (layer * N_EXPERTS + be[i], 0, 0))
    return pl.pallas_call(
        _expert_kernel,
        grid_spec=pltpu.PrefetchScalarGridSpec(
            num_scalar_prefetch=3, grid=(nb,),
            in_specs=[pl.BlockSpec(memory_space=pl.ANY), weight(D, D_EXPERT), weight(D, D_EXPERT), weight(D_EXPERT, D)],
            out_specs=pl.BlockSpec((blk_rows, LANES), lambda i, tok, be, nu: (i, 0)),
            scratch_shapes=[pltpu.VMEM((2, blk_rows, LANES), F32), pltpu.VMEM((D, D_EXPERT), BF16),
                            pltpu.VMEM((D, D_EXPERT), BF16), pltpu.VMEM((D_EXPERT, D), BF16),
                            pltpu.SemaphoreType.DMA((2,))]),
        out_shape=jax.ShapeDtypeStruct((nb * blk_rows, LANES), F32),
        compiler_params=_cparams("arbitrary"), name="moe_experts")(
            slot_tok, block_expert, n_used, h_tiles, w_gate, w_up, w_down)


def _layer_out_kernel(pos_ref, h_ref, hb_ref, p_ref, r_ref, yb_ref, wg_ref, wp_ref, g_ref, b_ref,
                      o_ref, ob_ref, buf_ref, sem):
    tm = h_ref.shape[0]
    base = pl.program_id(0) * tm

    def start(r, c):
        for k in range(TOP_K):
            _tile_copy(yb_ref, buf_ref.at[k], pos_ref[(base + r) * TOP_K + k], r, sem).start()
        return c

    def wait(r, c):
        for k in range(TOP_K):
            _tile_copy(yb_ref, buf_ref.at[k], 0, r, sem).wait()
        return c

    lax.fori_loop(0, tm, start, 0, unroll=8)
    ple = _sigmoid(_dot(hb_ref[...], wg_ref[...])) * _dot(p_ref[...].astype(BF16), wp_ref[...])
    lax.fori_loop(0, tm, wait, 0, unroll=8)
    route = r_ref[...]
    moe = (route[:, 2:3] * _load_token_tiles(buf_ref.at[0], tm)
           + route[:, 3:4] * _load_token_tiles(buf_ref.at[1], tm))
    x = _layer_norm(ALPHA * h_ref[...] + moe + ple, g_ref[...], b_ref[...])
    o_ref[...] = x
    ob_ref[...] = x.astype(BF16)


def layer_out(h, hb, p, route, pos, yb, w_ple_gate, w_ple_proj, ln_g, ln_b, tm=256):
    T, D = h.shape
    row = lambda n: pl.BlockSpec((tm, n), lambda i, pos: (i, 0))
    full = lambda a, b: pl.BlockSpec((a, b), lambda i, pos: (0, 0))
    return pl.pallas_call(
        _layer_out_kernel,
        grid_spec=pltpu.PrefetchScalarGridSpec(
            num_scalar_prefetch=1, grid=(T // tm,),
            in_specs=[row(D), row(D), row(PLE_DIM), row(LANES), pl.BlockSpec(memory_space=pl.ANY),
                      full(D, D), full(PLE_DIM, D), full(1, D), full(1, D)],
            out_specs=[row(D), row(D)],
            scratch_shapes=[pltpu.VMEM((TOP_K, tm * TOK_TILE, LANES), F32), pltpu.SemaphoreType.DMA(())]),
        out_shape=[jax.ShapeDtypeStruct((T, D), F32), jax.ShapeDtypeStruct((T, D), BF16)],
        compiler_params=_cparams("arbitrary"), name="moe_combine_ple_ln2")(
            pos, h, hb, p, route, yb, w_ple_gate, w_ple_proj, ln_g.reshape(1, D), ln_b.reshape(1, D))


def kernel(x_prompt, x_sample, p_prompt, p_sample, cache_k, cache_v, state_ssm_re, state_ssm_im, ln_in_g, ln_in_b, w_in, ssm_lam_re, ssm_lam_im, ssm_log_step, ssm_b_re, ssm_b_im, ssm_c_re, ssm_c_im, ssm_d, ssm_w_glu, ssm_b_glu, g_attn, g_ssm, w_out, ln1_g, ln1_b, w_router_group, b_router_group, w_router_expert, b_router_expert, w_exp_gate, w_exp_up, w_exp_down, w_ple_gate, w_ple_proj, ln2_g, ln2_b):
    B, L, D = x_prompt.shape
    SB, SL, _ = x_sample.shape
    depth = w_in.shape[0]
    past = cache_k.shape[2]
    TP, TS = B * L, SB * SL
    T = TP + TS
    n_blocks = (T * TOP_K) // MOE_ROWS + N_EXPERTS

    x, xb = ln_in(jnp.concatenate([x_prompt.reshape(TP, D), x_sample.reshape(TS, D)], axis=0), ln_in_g, ln_in_b)
    p_all = jnp.concatenate([p_prompt.reshape(depth, TP, PLE_DIM), p_sample.reshape(depth, TS, PLE_DIM)], axis=1)

    w_in_b = w_in.astype(BF16)
    w_out_b = w_out.astype(BF16)
    w_glu_b = ssm_w_glu.astype(BF16)
    w_pg_b = w_ple_gate.astype(BF16)
    w_pp_b = w_ple_proj.astype(BF16)
    w_route = jnp.pad(jnp.concatenate([w_router_group, w_router_expert], axis=-1),
                      ((0, 0), (0, 0), (0, LANES - N_EXPERT_GROUPS - N_EXPERTS)))
    b_route = jnp.pad(jnp.concatenate([b_router_group, b_router_expert], axis=-1),
                      ((0, 0), (0, LANES - N_EXPERT_GROUPS - N_EXPERTS))).reshape(depth, 1, LANES)
    a_re, a_im, bb_re, bb_im = ssm_params(ssm_lam_re, ssm_lam_im, ssm_log_step, ssm_b_re, ssm_b_im)

    cache_k_all = cache_k.reshape(depth * SB, past * N_HEADS, HEAD_DIM)
    cache_v_all = cache_v.reshape(depth * SB, past * N_HEADS, HEAD_DIM)
    w_eg = w_exp_gate.reshape(depth * N_EXPERTS, D, D_EXPERT)
    w_eu = w_exp_up.reshape(depth * N_EXPERTS, D, D_EXPERT)
    w_ed = w_exp_down.reshape(depth * N_EXPERTS, D_EXPERT, D)
    zeros_state = jnp.zeros((SUBLANES, SSM_LANES), F32)
    outs = [[] for _ in range(8)]
    for l in range(depth):
        proj = matmul(xb, w_in_b[l])
        attn_p = attn_prompt(proj, B, L)
        attn_s = attn_sample(proj, cache_k_all, cache_v_all, l, TP, SB, SL)

        wb, wc_re, wc_im = ssm_weights(bb_re[l], bb_im[l], ssm_c_re[l], ssm_c_im[l])
        lam_r = a_re[l].reshape(1, SSM_LANES)
        lam_i = a_im[l].reshape(1, SSM_LANES)
        d_row = ssm_d[l].reshape(1, W_SSM)
        ssm_p, hr_p, hi_p = ssm_mixer(proj, 0, B, L, zeros_state, zeros_state, lam_r, lam_i, wb, wc_re, wc_im,
                                      d_row, w_glu_b[l], ssm_b_glu[l], steps=64, whole=False)
        ssm_s, hr_s, hi_s = ssm_mixer(proj, TP, SB, SL, state_ssm_re[l].reshape(SB, SSM_LANES),
                                      state_ssm_im[l].reshape(SB, SSM_LANES), lam_r, lam_i, wb, wc_re, wc_im,
                                      d_row, w_glu_b[l], ssm_b_glu[l], steps=8, whole=True)
        ssm = jnp.concatenate([ssm_p.reshape(TP, W_SSM), ssm_s.reshape(TS, W_SSM)], axis=0)

        h, hb, h_tiles = mix_out(jnp.concatenate([attn_p, attn_s], axis=0), ssm, x, g_attn[l], g_ssm[l],
                                 w_out_b[l], ln1_g[l], ln1_b[l])
        route = router(h, w_route[l], b_route[l])
        slot_tok, pos, block_expert, n_used = dispatch_plan(route[:, :TOP_K].astype(jnp.int32), n_blocks)
        yb = expert_mlp(h_tiles, slot_tok, block_expert, n_used, w_eg, w_eu, w_ed, l)
        x, xb = layer_out(h, hb, p_all[l], route, pos, yb, w_pg_b[l], w_pp_b[l], ln2_g[l], ln2_b[l])

        kv = lambda rows, c, b, s: proj[rows, c * W_ATTN:(c + 1) * W_ATTN].reshape(b, s, N_HEADS, HEAD_DIM)
        G, P = N_SSM_GROUPS, SSM_STATE
        for lst, val in zip(outs, (kv(slice(0, TP), 1, B, L), kv(slice(0, TP), 2, B, L),
                                   hr_p[:B].reshape(B, G, P), hi_p[:B].reshape(B, G, P),
                                   kv(slice(TP, T), 1, SB, SL), kv(slice(TP, T), 2, SB, SL),
                                   hr_s.reshape(SB, G, P), hi_s.reshape(SB, G, P))):
            lst.append(val)

    return (x[:TP].reshape(B, L, D), x[TP:].reshape(SB, SL, D)) + tuple(jnp.stack(o) for o in outs)
```

```python
import functools
import math

import jax
import jax.numpy as jnp
from jax import lax
from jax.experimental import pallas as pl
from jax.experimental.pallas import tpu as pltpu

F32 = jnp.float32
BF16 = jnp.bfloat16

D_MODEL = 2048
DEPTH = 4
W_ATTN = 1024
W_SSM = 1024
HEAD_DIM = 128
N_HEADS = 8
SSM_GROUP = 16
N_SSM_GROUPS = 64
SSM_STATE = 64
N_EXPERT_GROUPS = 4
EXPERTS_PER_GROUP = 8
N_EXPERTS = 32
TOP_K = 2
D_EXPERT = 512
PLE_DIM = 256
ALPHA = (2 * DEPTH) ** 0.25
LN_EPS = 1e-5
LOG2_E = math.log2(math.e)

LANES = 128
SUBLANES = 8
VMEM_LIMIT = 56 * 1024 * 1024
SSM_LANES = N_SSM_GROUPS * SSM_STATE
SSM_PIECES = 8
ATT_BLK = 128
ATT_TQ = 512
ATT_UNROLL = 4
MOE_ROWS = 256
TOK_TILE = D_MODEL // LANES
DMA_PRIORITIES = 2


def _cparams(*sem):
    return pltpu.CompilerParams(dimension_semantics=sem, vmem_limit_bytes=VMEM_LIMIT)


def _dot(a, b):
    return jnp.dot(a, b, preferred_element_type=F32)


def _dot_nt(a, b):
    return lax.dot_general(a, b, (((1,), (1,)), ((), ())), preferred_element_type=F32)


def _split_bf16(x):
    hi = x.astype(BF16)
    lo = (x - hi.astype(F32)).astype(BF16)
    return hi, lo


def _layer_norm(x, g, b):
    mu = jnp.mean(x, -1, keepdims=True)
    xc = x - mu
    var = jnp.mean(xc * xc, -1, keepdims=True)
    return xc * lax.rsqrt(var + LN_EPS) * g + b


def _rms_norm(x, g):
    return x * lax.rsqrt(jnp.mean(x * x, -1, keepdims=True) + LN_EPS) * g


def _sigmoid(x):
    return 1.0 / (1.0 + jnp.exp(-x))


def _ln_in_kernel(x_ref, g_ref, b_ref, o_ref, ob_ref):
    y = _layer_norm(x_ref[...], g_ref[...], b_ref[...])
    o_ref[...] = y
    ob_ref[...] = y.astype(BF16)


def ln_in(x, g, b, tm=256):
    T, D = x.shape
    row = pl.BlockSpec((tm, D), lambda i: (i, 0))
    vec = pl.BlockSpec((1, D), lambda i: (0, 0))
    return pl.pallas_call(
        _ln_in_kernel, grid=(T // tm,), in_specs=[row, vec, vec], out_specs=[row, row],
        out_shape=[jax.ShapeDtypeStruct((T, D), F32), jax.ShapeDtypeStruct((T, D), BF16)],
        compiler_params=_cparams("parallel"), name="ln_in")(x, g.reshape(1, D), b.reshape(1, D))


def _matmul_kernel(x_ref, w_ref, o_ref):
    o_ref[...] = _dot(x_ref[...], w_ref[...])


def matmul(x, w, tm=1024, tn=512):
    M, K = x.shape
    N = w.shape[1]
    return pl.pallas_call(
        _matmul_kernel, grid=(M // tm, N // tn),
        in_specs=[pl.BlockSpec((tm, K), lambda i, j: (i, 0)), pl.BlockSpec((K, tn), lambda i, j: (0, j))],
        out_specs=pl.BlockSpec((tm, tn), lambda i, j: (i, j)),
        out_shape=jax.ShapeDtypeStruct((M, N), F32),
        compiler_params=_cparams("parallel", "arbitrary"), name="w_in_proj")(x, w)


def _suffix_sum_matrix(n, width):
    r = lax.broadcasted_iota(jnp.int32, (2 * n, 2 * width), 0)
    c = lax.broadcasted_iota(jnp.int32, (2 * n, 2 * width), 1)
    j = jnp.where(r >= n, r - n, r)
    later = jnp.logical_or(c < width, jnp.logical_and(c < width + n, j > c - width))
    return jnp.where(later, 1.0, 0.0).astype(BF16)


def _sb_weights(z, run, sums, n_vis, diag):
    rows, nk = z.shape
    bw = min(nk, ATT_BLK)
    nb = nk // bw
    blk = lambda a, b: a[:, b * bw:(b + 1) * bw]
    zs = [blk(z, b) for b in range(n_vis)]
    sps = [jnp.maximum(zb, 0.0) + jnp.log2(1.0 + jnp.exp2(-jnp.abs(zb))) for zb in zs]
    if diag:
        visible = (lax.broadcasted_iota(jnp.int32, (rows, bw), 1) < lax.broadcasted_iota(jnp.int32, (rows, bw), 0))
        sps[-1] = jnp.where(visible, sps[-1], 0.0)
    sums_of = [_dot(jnp.concatenate(_split_bf16(s), axis=-1), sums) for s in sps]
    ws = [jnp.zeros((rows, bw), BF16)] * nb
    for b in reversed(range(n_vis)):
        later = sums_of[b][:, LANES:LANES + bw]
        w = jnp.exp2(zs[b] - sps[b] - later - run[:, :bw])
        if diag and b == n_vis - 1:
            w = jnp.where(visible, w, 0.0)
        ws[b] = w.astype(BF16)
        run = run + sums_of[b][:, :LANES]
    return jnp.concatenate(ws, axis=-1), run


def _attn_prompt_kernel(q_ref, k_ref, v_ref, o_ref, kb_ref, vb_ref, acc_ref, run_ref, z_ref, w_ref):
    qi = pl.program_id(2)
    nsub = ATT_TQ // ATT_BLK

    @pl.when(qi == 0)
    def _():
        kb_ref[...] = k_ref[...].astype(BF16)
        vb_ref[...] = v_ref[...].astype(BF16)

    sums = _suffix_sum_matrix(ATT_BLK, LANES)
    qs = (q_ref[...] * (HEAD_DIM ** -0.5 * LOG2_E)).astype(BF16)
    sub = lambda s: slice(s * ATT_BLK, (s + 1) * ATT_BLK)

    def scores(chunk, s):
        start = pl.multiple_of(chunk * ATT_TQ, ATT_TQ)
        z_ref[s] = _dot_nt(qs[sub(s)], kb_ref[pl.ds(start, ATT_TQ), :])

    def values(chunk, s):
        start = pl.multiple_of(chunk * ATT_TQ, ATT_TQ)
        return _dot(w_ref[s], vb_ref[pl.ds(start, ATT_TQ), :])

    for s in range(nsub):
        scores(qi, s)

    for s in range(nsub):
        z = z_ref[s]
        scores(jnp.maximum(qi - 1, 0), s)
        w_ref[s], run_ref[sub(s), :] = _sb_weights(z, jnp.zeros((ATT_BLK, LANES), F32), sums, s + 1, True)
    acc_ref[...] = jnp.zeros_like(acc_ref)

    def body(it, c):
        for s in range(nsub):
            z = z_ref[s]
            scores(jnp.maximum(qi - 2 - it, 0), s)
            pv = values(qi - it, s)
            w_ref[s], run_ref[sub(s), :] = _sb_weights(z, run_ref[sub(s), :], sums, nsub, False)
            acc_ref[sub(s), :] += pv
        return c

    lax.fori_loop(0, qi, body, 0)
    for s in range(nsub):
        o_ref[sub(s), :] = acc_ref[sub(s), :] + values(0, s)


def attn_prompt(proj, batch, seq):
    nq = seq // ATT_TQ
    q_spec = pl.BlockSpec((ATT_TQ, HEAD_DIM), lambda b, h, i: (b * nq + i, h))
    k_spec = pl.BlockSpec((seq, HEAD_DIM), lambda b, h, i: (b, N_HEADS + h))
    v_spec = pl.BlockSpec((seq, HEAD_DIM), lambda b, h, i: (b, 2 * N_HEADS + h))
    return pl.pallas_call(
        _attn_prompt_kernel, grid=(batch, N_HEADS, nq), in_specs=[q_spec, k_spec, v_spec],
        out_specs=q_spec, out_shape=jax.ShapeDtypeStruct((batch * seq, W_ATTN), F32),
        scratch_shapes=[pltpu.VMEM((seq, HEAD_DIM), BF16), pltpu.VMEM((seq, HEAD_DIM), BF16),
                        pltpu.VMEM((ATT_TQ, HEAD_DIM), F32), pltpu.VMEM((ATT_TQ, LANES), F32),
                        pltpu.VMEM((ATT_TQ // ATT_BLK, ATT_BLK, ATT_TQ), F32),
                        pltpu.VMEM((ATT_TQ // ATT_BLK, ATT_BLK, ATT_TQ), BF16)],
        compiler_params=_cparams("parallel", "parallel", "arbitrary"), name="attn_prompt")(proj, proj, proj)


def _attn_sample_kernel(q_ref, k_ref, v_ref, ck_ref, cv_ref, o_ref, qs_ref, kn_ref, vn_ref, oh_ref, *, past, tq):
    for h in range(N_HEADS):
        cols = slice(h * HEAD_DIM, (h + 1) * HEAD_DIM)
        qs_ref[h] = (q_ref[:, cols] * (HEAD_DIM ** -0.5 * LOG2_E)).astype(BF16)
        kn_ref[h] = k_ref[:, cols].astype(BF16)
        vn_ref[h] = v_ref[:, cols].astype(BF16)
    sums_new = _suffix_sum_matrix(tq, LANES)
    sums = _suffix_sum_matrix(ATT_BLK, LANES)

    def head(h):
        qs = qs_ref[h]
        w_new, run = _sb_weights(_dot_nt(qs, kn_ref[h]), jnp.zeros((tq, LANES), F32), sums_new, 1, True)
        cached = pl.ds(h, past, stride=N_HEADS)
        z = _dot_nt(qs, ck_ref[cached, :].astype(BF16))
        w_old, _ = _sb_weights(z, run, sums, past // ATT_BLK, False)
        oh_ref[h] = _dot(w_new, vn_ref[h]) + _dot(w_old, cv_ref[cached, :].astype(BF16))

    def head_pair(i, c):
        head(2 * i)
        head(2 * i + 1)
        return c

    lax.fori_loop(0, N_HEADS // 2, head_pair, 0)
    for h in range(N_HEADS):
        o_ref[:, h * HEAD_DIM:(h + 1) * HEAD_DIM] = oh_ref[h]


def attn_sample(proj, cache_k, cache_v, layer, row0, batch, tq):
    past = cache_k.shape[1] // N_HEADS
    rb0 = row0 // tq
    new = lambda c: pl.BlockSpec((tq, W_ATTN), lambda b: (rb0 + b, c))
    cache = pl.BlockSpec((None, past * N_HEADS, HEAD_DIM), lambda b: (layer * batch + b, 0, 0))
    per_head = lambda dt: pltpu.VMEM((N_HEADS, tq, HEAD_DIM), dt)
    return pl.pallas_call(
        functools.partial(_attn_sample_kernel, past=past, tq=tq), grid=(batch,),
        in_specs=[new(0), new(1), new(2), cache, cache],
        out_specs=pl.BlockSpec((tq, W_ATTN), lambda b: (b, 0)),
        out_shape=jax.ShapeDtypeStruct((batch * tq, W_ATTN), F32),
        scratch_shapes=[per_head(BF16), per_head(BF16), per_head(BF16), per_head(F32)],
        compiler_params=_cparams("parallel"), name="attn_sample")(proj, proj, proj, cache_k, cache_v)


def _ssm_param_kernel(lr_ref, li_ref, ls_ref, br_ref, bi_ref, ar_ref, ai_ref, bbr_ref, bbi_ref):
    lam_re = jnp.minimum(lr_ref[0], -1e-4)
    lam_im = li_ref[0]
    step = jnp.exp(ls_ref[0])
    mag = jnp.exp(lam_re * step)
    a_re = mag * jnp.cos(lam_im * step)
    a_im = mag * jnp.sin(lam_im * step)
    ar_ref[0] = a_re
    ai_ref[0] = a_im
    den = lam_re * lam_re + lam_im * lam_im
    nr = a_re - 1.0
    c_re = (nr * lam_re + a_im * lam_im) / den
    c_im = (a_im * lam_re - nr * lam_im) / den
    for c in range(SSM_GROUP):
        b_re = br_ref[0, c]
        b_im = bi_ref[0, c]
        bbr_ref[0, c] = c_re * b_re - c_im * b_im
        bbi_ref[0, c] = c_re * b_im + c_im * b_re


def ssm_params(lam_re, lam_im, log_step, b_re, b_im):
    G, P, C = N_SSM_GROUPS, SSM_STATE, SSM_GROUP
    gp = pl.BlockSpec((1, G, P), lambda l: (l, 0, 0))
    cgp = pl.BlockSpec((1, C, G, P), lambda l: (l, 0, 0, 0))
    return pl.pallas_call(
        _ssm_param_kernel, grid=(DEPTH,),
        in_specs=[gp, gp, pl.BlockSpec((1, G, 1), lambda l: (l, 0, 0)), cgp, cgp],
        out_specs=[gp, gp, cgp, cgp],
        out_shape=[jax.ShapeDtypeStruct((DEPTH, G, P), F32)] * 2 + [jax.ShapeDtypeStruct((DEPTH, C, G, P), F32)] * 2,
        compiler_params=_cparams("parallel"), name="ssm_params")(
            lam_re, lam_im, log_step.reshape(DEPTH, G, 1),
            b_re.transpose(0, 3, 1, 2), b_im.transpose(0, 3, 1, 2))


def _gelu_tanh(y):
    return 0.5 * y * (1.0 + jnp.tanh(math.sqrt(2.0 / math.pi) * (y + 0.044715 * (y * y * y))))


def _ssm_kernel(*refs, n_seq, rows, steps, seq_len, whole):
    n_io = 1 if whole else n_seq
    u_refs = refs[:n_io]
    (h0r_ref, h0i_ref, ar_ref, ai_ref, wb_ref, wcr_ref, wci_ref, d_ref, wg_ref, bg_ref) = refs[n_io:n_io + 10]
    o_ref, sr_ref, si_ref, us_ref, hr_ref, hi_ref, zs_ref = refs[n_io + 10:]
    t = pl.program_id(0)
    piece_in = SSM_LANES // SSM_PIECES
    piece_ch = W_SSM // SSM_PIECES
    lane_blk = 4 * SUBLANES * LANES // rows

    def seq_rows(b):
        return pl.ds(pl.multiple_of(b * seq_len + t * steps, steps), steps)

    @pl.when(t == 0)
    def _():
        sr_ref[...] = h0r_ref[...]
        si_ref[...] = h0i_ref[...]
        if n_seq < rows:
            us_ref[...] = jnp.zeros_like(us_ref)

    for b in range(n_seq):
        ub = u_refs[0][seq_rows(b), :] if whole else u_refs[b][...]
        for k in range(SSM_PIECES):
            us_ref[k, pl.ds(b, steps, stride=rows), :] = ub[:, k * piece_ch:(k + 1) * piece_ch]

    for k in range(SSM_PIECES):
        bu = _dot(us_ref[k].astype(BF16), wb_ref[k])
        hr_ref[:, k * piece_in:(k + 1) * piece_in] = bu[:, :piece_in]
        hi_ref[:, k * piece_in:(k + 1) * piece_in] = bu[:, piece_in:]

    for lb in range(SSM_LANES // lane_blk):
        ln = slice(lb * lane_blk, (lb + 1) * lane_blk)
        a_re = jnp.broadcast_to(ar_ref[:, ln], (rows, lane_blk))
        a_im = jnp.broadcast_to(ai_ref[:, ln], (rows, lane_blk))

        def step(j, carry, ln=ln, a_re=a_re, a_im=a_im):
            h_re, h_im = carry
            r0 = pl.multiple_of(j * rows, rows)
            n_re = a_re * h_re - a_im * h_im + hr_ref[pl.ds(r0, rows), ln]
            n_im = a_re * h_im + a_im * h_re + hi_ref[pl.ds(r0, rows), ln]
            hr_ref[pl.ds(r0, rows), ln] = n_re
            hi_ref[pl.ds(r0, rows), ln] = n_im
            return n_re, n_im

        h_re, h_im = lax.fori_loop(0, steps, step, (sr_ref[:, ln], si_ref[:, ln]), unroll=4)
        sr_ref[:, ln] = h_re
        si_ref[:, ln] = h_im

    zs = []
    for k in range(SSM_PIECES):
        ln = slice(k * piece_in, (k + 1) * piece_in)
        y = _dot(hr_ref[:, ln].astype(BF16), wcr_ref[k]) - _dot(hi_ref[:, ln].astype(BF16), wci_ref[k])
        zs.append(_gelu_tanh(y + d_ref[:, k * piece_ch:(k + 1) * piece_ch] * us_ref[k]))
    z = jnp.concatenate(zs, axis=-1)
    out = z * _sigmoid(_dot(z.astype(BF16), wg_ref[...]) + bg_ref[...])
    for k in range(SSM_PIECES):
        zs_ref[k] = out[:, k * piece_ch:(k + 1) * piece_ch]

    for b in range(n_seq):
        ob = jnp.concatenate([zs_ref[k, pl.ds(b, steps, stride=rows), :] for k in range(SSM_PIECES)], axis=-1)
        if whole:
            o_ref[seq_rows(b), :] = ob
        else:
            o_ref[b] = ob


def ssm_mixer(proj, row0, n_seq, seq_len, h0_re, h0_im, a_re, a_im, wb, wc_re, wc_im, d, w_glu, b_glu, steps, whole):
    rows = h0_re.shape[0]
    tile = rows * steps
    u_col = proj.shape[1] // W_SSM - 1
    full = lambda *s: pl.BlockSpec(s, lambda t: (0,) * len(s))
    if whole:
        u_specs = [pl.BlockSpec((n_seq * seq_len, W_SSM), lambda t: (row0 // (n_seq * seq_len), u_col))]
        o_spec = full(n_seq * seq_len, W_SSM)
        o_shape = jax.ShapeDtypeStruct((n_seq * seq_len, W_SSM), F32)
    else:
        u_specs = [pl.BlockSpec((steps, W_SSM), lambda t, b=b: (b * (seq_len // steps) + t, u_col)) for b in range(n_seq)]
        o_spec = pl.BlockSpec((n_seq, steps, W_SSM), lambda t: (0, t, 0))
        o_shape = jax.ShapeDtypeStruct((n_seq, seq_len, W_SSM), F32)
    state = full(rows, SSM_LANES)
    out, s_re, s_im = pl.pallas_call(
        functools.partial(_ssm_kernel, n_seq=n_seq, rows=rows, steps=steps, seq_len=seq_len, whole=whole),
        grid=(seq_len // steps,),
        in_specs=u_specs + [state, state, full(1, SSM_LANES), full(1, SSM_LANES), full(*wb.shape),
                            full(*wc_re.shape), full(*wc_im.shape), full(1, W_SSM), full(W_SSM, W_SSM), full(1, W_SSM)],
        out_specs=[o_spec, state, state],
        out_shape=[o_shape] + [jax.ShapeDtypeStruct((rows, SSM_LANES), F32)] * 2,
        scratch_shapes=[pltpu.VMEM((SSM_PIECES, tile, LANES), F32), pltpu.VMEM((tile, SSM_LANES), F32),
                        pltpu.VMEM((tile, SSM_LANES), F32), pltpu.VMEM((SSM_PIECES, tile, LANES), F32)],
        compiler_params=_cparams("arbitrary"), name="ssm_mixer")(
            *([proj] * len(u_specs)), h0_re, h0_im, a_re, a_im, wb, wc_re, wc_im, d, w_glu, b_glu.reshape(1, W_SSM))
    return out.reshape(n_seq, seq_len, W_SSM), s_re, s_im


def _block_diag(w):
    k, g, a, b = w.shape
    eye = jnp.eye(g, dtype=w.dtype)
    return (w[:, :, :, None, :] * eye[None, :, None, :, None]).reshape(k, g * a, g * b)


def ssm_weights(bb_re, bb_im, c_re, c_im):
    gpp = N_SSM_GROUPS // SSM_PIECES
    to_in = lambda b: b.transpose(1, 0, 2).reshape(SSM_PIECES, gpp, SSM_GROUP, SSM_STATE)
    wb = jnp.concatenate([_block_diag(to_in(bb_re)), _block_diag(to_in(bb_im))], axis=-1)
    to_out = lambda c: c.transpose(0, 2, 1).reshape(SSM_PIECES, gpp, SSM_STATE, SSM_GROUP)
    return wb.astype(BF16), _block_diag(to_out(c_re)).astype(BF16), _block_diag(to_out(c_im)).astype(BF16)


def _store_token_tiles(ref, x):
    for c in range(TOK_TILE):
        ref[pl.ds(c, x.shape[0], stride=TOK_TILE), :] = x[:, c * LANES:(c + 1) * LANES]


def _load_token_tiles(ref, rows):
    return jnp.concatenate([ref[pl.ds(c, rows, stride=TOK_TILE), :] for c in range(TOK_TILE)], axis=-1)


def _mix_out_kernel(ap_ref, as_ref, sp_ref, ss_ref, x_ref, ga_ref, gs_ref, w_ref, g_ref, b_ref,
                    h_ref, hb_ref, ht_ref, *, prompt_tiles):
    is_prompt = pl.program_id(0) < prompt_tiles
    a = _rms_norm(jnp.where(is_prompt, ap_ref[...], as_ref[...]), ga_ref[...]).astype(BF16)
    s = _rms_norm(jnp.where(is_prompt, sp_ref[...], ss_ref[...]), gs_ref[...]).astype(BF16)
    y = _dot(a, w_ref[:W_ATTN, :]) + _dot(s, w_ref[W_ATTN:, :])
    h = _layer_norm(ALPHA * x_ref[...] + y, g_ref[...], b_ref[...])
    h_ref[...] = h
    hb_ref[...] = h.astype(BF16)
    _store_token_tiles(ht_ref, h)


def mix_out(attn_p, attn_s, ssm_p, ssm_s, x, g_attn, g_ssm, w_out, ln_g, ln_b, tm=256):
    T, D = x.shape
    pt = attn_p.shape[0] // tm
    st = attn_s.shape[0] // tm
    prompt = pl.BlockSpec((tm, W_ATTN), lambda i: (jnp.minimum(i, pt - 1), 0))
    sample = pl.BlockSpec((tm, W_ATTN), lambda i: (jnp.clip(i - pt, 0, st - 1), 0))
    row = pl.BlockSpec((tm, D), lambda i: (i, 0))
    vec = lambda n: pl.BlockSpec((1, n), lambda i: (0, 0))
    return pl.pallas_call(
        functools.partial(_mix_out_kernel, prompt_tiles=pt), grid=(T // tm,),
        in_specs=[prompt, sample, prompt, sample, row, vec(W_ATTN), vec(W_SSM),
                  pl.BlockSpec((D, D), lambda i: (0, 0)), vec(D), vec(D)],
        out_specs=[row, row, pl.BlockSpec((tm * TOK_TILE, LANES), lambda i: (i, 0))],
        out_shape=[jax.ShapeDtypeStruct((T, D), F32), jax.ShapeDtypeStruct((T, D), BF16),
                   jax.ShapeDtypeStruct((T * TOK_TILE, LANES), F32)],
        compiler_params=_cparams("arbitrary"), name="mix_out_ln1")(
            attn_p, attn_s, ssm_p, ssm_s, x, g_attn.reshape(1, -1), g_ssm.reshape(1, -1), w_out,
            ln_g.reshape(1, D), ln_b.reshape(1, D))


def _router_kernel(h_ref, w_ref, b_ref, o_ref):
    tm = h_ref.shape[0]
    hh, hl = _split_bf16(h_ref[...])
    wh, wl = _split_bf16(w_ref[...])
    logits = _dot(hh, wh) + _dot(hh, wl) + _dot(hl, wh) + b_ref[...]
    lane = lax.broadcasted_iota(jnp.int32, (tm, LANES), 1).astype(F32)
    neg = -jnp.inf
    big = float(LANES)

    def first_max(v):
        m = jnp.max(v, -1, keepdims=True)
        return m, jnp.min(jnp.where(v == m, lane, big), -1, keepdims=True)

    is_group = lane < N_EXPERT_GROUPS
    g_max, g_idx = first_max(jnp.where(is_group, logits, neg))
    pg_top = 1.0 / jnp.sum(jnp.where(is_group, jnp.exp(logits - g_max), 0.0), -1, keepdims=True)
    lo = N_EXPERT_GROUPS + EXPERTS_PER_GROUP * g_idx
    le = jnp.where(lane >= lo, jnp.where(lane < lo + EXPERTS_PER_GROUP, logits, neg), neg)
    m1, i1 = first_max(le)
    m2, i2 = first_max(jnp.where(lane == i1, neg, le))
    e2 = jnp.exp(m2 - m1)
    den = 1.0 + e2
    o_ref[...] = jnp.where(lane == 0, i1 - N_EXPERT_GROUPS,
                 jnp.where(lane == 1, i2 - N_EXPERT_GROUPS,
                 jnp.where(lane == 2, pg_top / den,
                 jnp.where(lane == 3, pg_top * e2 / den, 0.0))))


def router(h, w_cat, b_cat, tm=512):
    T, D = h.shape
    return pl.pallas_call(
        _router_kernel, grid=(T // tm,),
        in_specs=[pl.BlockSpec((tm, D), lambda i: (i, 0)), pl.BlockSpec((D, LANES), lambda i: (0, 0)),
                  pl.BlockSpec((1, LANES), lambda i: (0, 0))],
        out_specs=pl.BlockSpec((tm, LANES), lambda i: (i, 0)),
        out_shape=jax.ShapeDtypeStruct((T, LANES), F32),
        compiler_params=_cparams("parallel"), name="router")(h, w_cat, b_cat)


def dispatch_plan(expert_ids, n_blocks):
    flat = expert_ids.reshape(-1)
    A = flat.shape[0]
    onehot = (flat[:, None] == jnp.arange(N_EXPERTS, dtype=jnp.int32)[None, :]).reshape(A // LANES, LANES, N_EXPERTS)
    lower = (jnp.arange(LANES)[:, None] >= jnp.arange(LANES)[None, :]).astype(BF16)
    within = jnp.einsum('ij,cje->cie', lower, onehot.astype(BF16), preferred_element_type=F32)
    chunk_total = within[:, -1, :]
    chunk_start = jnp.cumsum(chunk_total, axis=0) - chunk_total
    rank = jnp.sum(jnp.where(onehot, within - 1.0 + chunk_start[:, None, :], 0.0), axis=-1).reshape(A).astype(jnp.int32)
    counts = (chunk_start[-1] + chunk_total[-1]).astype(jnp.int32)
    blocks = (counts + MOE_ROWS - 1) // MOE_ROWS
    blk_end = jnp.cumsum(blocks)
    pad_start = (blk_end - blocks) * MOE_ROWS
    pos = pad_start[flat] + rank
    slot_tok = jnp.zeros((n_blocks * MOE_ROWS,), jnp.int32).at[pos].set(jnp.arange(A, dtype=jnp.int32) // TOP_K)
    n_used = blk_end[-1]
    blk = jnp.minimum(jnp.arange(n_blocks, dtype=jnp.int32), n_used - 1)
    block_expert = jnp.minimum(jnp.searchsorted(blk_end, blk, side='right'), N_EXPERTS - 1).astype(jnp.int32)
    return slot_tok, pos.astype(jnp.int32), block_expert, n_used.reshape(1).astype(jnp.int32)


def _tile_copy(src_ref, dst_ref, src_tok, dst_tok, sem):
    return pltpu.make_async_copy(src_ref.at[pl.ds(src_tok * TOK_TILE, TOK_TILE)],
                                 dst_ref.at[pl.ds(dst_tok * TOK_TILE, TOK_TILE)], sem)


def _expert_kernel(tok_ref, be_ref, nu_ref, ht_ref, wg_ref, wu_ref, wd_ref, o_ref,
                   xbuf_ref, wgb_ref, wub_ref, wdb_ref, sem):
    i = pl.program_id(0)
    n_used = nu_ref[0]
    slot = lax.rem(i, 2)

    def start_gather(block, s):
        def body(r2, c):
            for prio in range(DMA_PRIORITIES):
                r = r2 * DMA_PRIORITIES + prio
                _tile_copy(ht_ref, xbuf_ref.at[s], tok_ref[block * MOE_ROWS + r], r, sem.at[s]).start(priority=prio)
            return c
        lax.fori_loop(0, MOE_ROWS // DMA_PRIORITIES, body, 0, unroll=4)

    def wait_gather(s):
        def body(r, c):
            _tile_copy(ht_ref, xbuf_ref.at[s], 0, r, sem.at[s]).wait()
            return c
        lax.fori_loop(0, MOE_ROWS, body, 0, unroll=8)

    @pl.when(i == 0)
    def _():
        start_gather(0, 0)

    @pl.when(i + 1 < n_used)
    def _():
        start_gather(i + 1, 1 - slot)

    @pl.when(jnp.logical_or(i == 0, be_ref[i] != be_ref[jnp.maximum(i - 1, 0)]))
    def _():
        wgb_ref[...] = wg_ref[0].astype(BF16)
        wub_ref[...] = wu_ref[0].astype(BF16)
        wdb_ref[...] = wd_ref[0].astype(BF16)

    @pl.when(i < n_used)
    def _():
        wait_gather(slot)
        x = _load_token_tiles(xbuf_ref.at[slot], MOE_ROWS).astype(BF16)
        g = _dot(x, wgb_ref[...])
        u = _dot(x, wub_ref[...])
        hid = (g * _sigmoid(g)) * u
        _store_token_tiles(o_ref, _dot(hid.astype(BF16), wdb_ref[...]))

    @pl.when(i >= n_used)
    def _():
        o_ref[...] = jnp.zeros_like(o_ref)


def expert_mlp(h_tiles, slot_tok, block_expert, n_used, w_gate, w_up, w_down, layer):
    nb = block_expert.shape[0]
    blk_rows = MOE_ROWS * TOK_TILE
    D = D_MODEL
    weight = lambda a, b: pl.BlockSpec((1, a, b), lambda i, tok, be, nu: (layer * N_EXPERTS + be[i], 0, 0))
    return pl.pallas_call(
        _expert_kernel,
        grid_spec=pltpu.PrefetchScalarGridSpec(
            num_scalar_prefetch=3, grid=(nb,),
            in_specs=[pl.BlockSpec(memory_space=pl.ANY), weight(D, D_EXPERT), weight(D, D_EXPERT), weight(D_EXPERT, D)],
            out_specs=pl.BlockSpec((blk_rows, LANES), lambda i, tok, be, nu: (i, 0)),
            scratch_shapes=[pltpu.VMEM((2, blk_rows, LANES), F32), pltpu.VMEM((D, D_EXPERT), BF16),
                            pltpu.VMEM((D, D_EXPERT), BF16), pltpu.VMEM((D_EXPERT, D), BF16),
                            pltpu.SemaphoreType.DMA((2,))]),
        out_shape=jax.ShapeDtypeStruct((nb * blk_rows, LANES), F32),
        compiler_params=_cparams("arbitrary"), name="moe_experts")(
            slot_tok, block_expert, n_used, h_tiles, w_gate, w_up, w_down)


def _layer_out_kernel(pos_ref, h_ref, hb_ref, p_ref, r_ref, yb_ref, wg_ref, wp_ref, g_ref, b_ref,
                      o_ref, ob_ref, buf_ref, sem):
    tm = h_ref.shape[0]
    base = pl.program_id(0) * tm

    def start(r, c):
        for k in range(TOP_K):
            _tile_copy(yb_ref, buf_ref.at[k], pos_ref[(base + r) * TOP_K + k], r, sem).start(priority=k % DMA_PRIORITIES)
        return c

    def wait(r, c):
        for k in range(TOP_K):
            _tile_copy(yb_ref, buf_ref.at[k], 0, r, sem).wait()
        return c

    lax.fori_loop(0, tm, start, 0, unroll=8)
    ple = _sigmoid(_dot(hb_ref[...], wg_ref[...])) * _dot(p_ref[...].astype(BF16), wp_ref[...])
    lax.fori_loop(0, tm, wait, 0, unroll=8)
    route = r_ref[...]
    moe = (route[:, 2:3] * _load_token_tiles(buf_ref.at[0], tm)
           + route[:, 3:4] * _load_token_tiles(buf_ref.at[1], tm))
    x = _layer_norm(ALPHA * h_ref[...] + moe + ple, g_ref[...], b_ref[...])
    o_ref[...] = x
    ob_ref[...] = x.astype(BF16)


def layer_out(h, hb, p, route, pos, yb, w_ple_gate, w_ple_proj, ln_g, ln_b, tm=256):
    T, D = h.shape
    row = lambda n: pl.BlockSpec((tm, n), lambda i, pos: (i, 0))
    full = lambda a, b: pl.BlockSpec((a, b), lambda i, pos: (0, 0))
    return pl.pallas_call(
        _layer_out_kernel,
        grid_spec=pltpu.PrefetchScalarGridSpec(
            num_scalar_prefetch=1, grid=(T // tm,),
            in_specs=[row(D), row(D), row(PLE_DIM), row(LANES), pl.BlockSpec(memory_space=pl.ANY),
                      full(D, D), full(PLE_DIM, D), full(1, D), full(1, D)],
            out_specs=[row(D), row(D)],
            scratch_shapes=[pltpu.VMEM((TOP_K, tm * TOK_TILE, LANES), F32), pltpu.SemaphoreType.DMA(())]),
        out_shape=[jax.ShapeDtypeStruct((T, D), F32), jax.ShapeDtypeStruct((T, D), BF16)],
        compiler_params=_cparams("arbitrary"), name="moe_combine_ple_ln2")(
            pos, h, hb, p, route, yb, w_ple_gate, w_ple_proj, ln_g.reshape(1, D), ln_b.reshape(1, D))


def kernel(x_prompt, x_sample, p_prompt, p_sample, cache_k, cache_v, state_ssm_re, state_ssm_im, ln_in_g, ln_in_b, w_in, ssm_lam_re, ssm_lam_im, ssm_log_step, ssm_b_re, ssm_b_im, ssm_c_re, ssm_c_im, ssm_d, ssm_w_glu, ssm_b_glu, g_attn, g_ssm, w_out, ln1_g, ln1_b, w_router_group, b_router_group, w_router_expert, b_router_expert, w_exp_gate, w_exp_up, w_exp_down, w_ple_gate, w_ple_proj, ln2_g, ln2_b):
    B, L, D = x_prompt.shape
    SB, SL, _ = x_sample.shape
    depth = w_in.shape[0]
    past = cache_k.shape[2]
    TP, TS = B * L, SB * SL
    T = TP + TS
    n_blocks = (T * TOP_K) // MOE_ROWS + N_EXPERTS

    x, xb = ln_in(jnp.concatenate([x_prompt.reshape(TP, D), x_sample.reshape(TS, D)], axis=0), ln_in_g, ln_in_b)
    p_all = jnp.concatenate([p_prompt.reshape(depth, TP, PLE_DIM), p_sample.reshape(depth, TS, PLE_DIM)], axis=1)

    w_in_b = w_in.astype(BF16)
    w_out_b = w_out.astype(BF16)
    w_glu_b = ssm_w_glu.astype(BF16)
    w_pg_b = w_ple_gate.astype(BF16)
    w_pp_b = w_ple_proj.astype(BF16)
    w_route = jnp.pad(jnp.concatenate([w_router_group, w_router_expert], axis=-1),
                      ((0, 0), (0, 0), (0, LANES - N_EXPERT_GROUPS - N_EXPERTS)))
    b_route = jnp.pad(jnp.concatenate([b_router_group, b_router_expert], axis=-1),
                      ((0, 0), (0, LANES - N_EXPERT_GROUPS - N_EXPERTS))).reshape(depth, 1, LANES)
    a_re, a_im, bb_re, bb_im = ssm_params(ssm_lam_re, ssm_lam_im, ssm_log_step, ssm_b_re, ssm_b_im)

    cache_k_all = cache_k.reshape(depth * SB, past * N_HEADS, HEAD_DIM)
    cache_v_all = cache_v.reshape(depth * SB, past * N_HEADS, HEAD_DIM)
    w_eg = w_exp_gate.reshape(depth * N_EXPERTS, D, D_EXPERT)
    w_eu = w_exp_up.reshape(depth * N_EXPERTS, D, D_EXPERT)
    w_ed = w_exp_down.reshape(depth * N_EXPERTS, D_EXPERT, D)
    zeros_state = jnp.zeros((SUBLANES, SSM_LANES), F32)
    outs = [[] for _ in range(8)]
    for l in range(depth):
        proj = matmul(xb, w_in_b[l])
        attn_p = attn_prompt(proj, B, L)
        attn_s = attn_sample(proj, cache_k_all, cache_v_all, l, TP, SB, SL)

        wb, wc_re, wc_im = ssm_weights(bb_re[l], bb_im[l], ssm_c_re[l], ssm_c_im[l])
        lam_r = a_re[l].reshape(1, SSM_LANES)
        lam_i = a_im[l].reshape(1, SSM_LANES)
        d_row = ssm_d[l].reshape(1, W_SSM)
        ssm_p, hr_p, hi_p = ssm_mixer(proj, 0, B, L, zeros_state, zeros_state, lam_r, lam_i, wb, wc_re, wc_im,
                                      d_row, w_glu_b[l], ssm_b_glu[l], steps=64, whole=False)
        ssm_s, hr_s, hi_s = ssm_mixer(proj, TP, SB, SL, state_ssm_re[l].reshape(SB, SSM_LANES),
                                      state_ssm_im[l].reshape(SB, SSM_LANES), lam_r, lam_i, wb, wc_re, wc_im,
                                      d_row, w_glu_b[l], ssm_b_glu[l], steps=8, whole=True)
        h, hb, h_tiles = mix_out(attn_p, attn_s, ssm_p.reshape(TP, W_SSM), ssm_s.reshape(TS, W_SSM), x,
                                 g_attn[l], g_ssm[l], w_out_b[l], ln1_g[l], ln1_b[l])
        route = router(h, w_route[l], b_route[l])
        slot_tok, pos, block_expert, n_used = dispatch_plan(route[:, :TOP_K].astype(jnp.int32), n_blocks)
        yb = expert_mlp(h_tiles, slot_tok, block_expert, n_used, w_eg, w_eu, w_ed, l)
        x, xb = layer_out(h, hb, p_all[l], route, pos, yb, w_pg_b[l], w_pp_b[l], ln2_g[l], ln2_b[l])

        kv = lambda rows, c, b, s: proj[rows, c * W_ATTN:(c + 1) * W_ATTN].reshape(b, s, N_HEADS, HEAD_DIM)
        G, P = N_SSM_GROUPS, SSM_STATE
        for lst, val in zip(outs, (kv(slice(0, TP), 1, B, L), kv(slice(0, TP), 2, B, L),
                                   hr_p[:B].reshape(B, G, P), hi_p[:B].reshape(B, G, P),
                                   kv(slice(TP, T), 1, SB, SL), kv(slice(TP, T), 2, SB, SL),
                                   hr_s.reshape(SB, G, P), hi_s.reshape(SB, G, P))):
            lst.append(val)

    return (x[:TP].reshape(B, L, D), x[TP:].reshape(SB, SL, D)) + tuple(jnp.stack(o) for o in outs)
```

```python
import functools
import math

import jax
import jax.numpy as jnp
from jax import lax
from jax.experimental import pallas as pl
from jax.experimental.pallas import tpu as pltpu

F32 = jnp.float32
BF16 = jnp.bfloat16

D_MODEL = 2048
DEPTH = 4
W_ATTN = 1024
W_SSM = 1024
HEAD_DIM = 128
N_HEADS = 8
SSM_GROUP = 16
N_SSM_GROUPS = 64
SSM_STATE = 64
N_EXPERT_GROUPS = 4
EXPERTS_PER_GROUP = 8
N_EXPERTS = 32
TOP_K = 2
D_EXPERT = 512
PLE_DIM = 256
ALPHA = (2 * DEPTH) ** 0.25
LN_EPS = 1e-5
LOG2_E = math.log2(math.e)

LANES = 128
SUBLANES = 8
VMEM_LIMIT = 56 * 1024 * 1024
SSM_LANES = N_SSM_GROUPS * SSM_STATE
SSM_PIECES = 8
ATT_BLK = 128
ATT_TQ = 512
ATT_UNROLL = 4
MOE_ROWS = 256
TOK_TILE = D_MODEL // LANES
DMA_PRIORITIES = 2


def _cparams(*sem):
    return pltpu.CompilerParams(dimension_semantics=sem, vmem_limit_bytes=VMEM_LIMIT)


def _dot(a, b):
    return jnp.dot(a, b, preferred_element_type=F32)


def _dot_nt(a, b):
    return lax.dot_general(a, b, (((1,), (1,)), ((), ())), preferred_element_type=F32)


def _split_bf16(x):
    hi = x.astype(BF16)
    lo = (x - hi.astype(F32)).astype(BF16)
    return hi, lo


def _layer_norm(x, g, b):
    mu = jnp.mean(x, -1, keepdims=True)
    xc = x - mu
    var = jnp.mean(xc * xc, -1, keepdims=True)
    return xc * lax.rsqrt(var + LN_EPS) * g + b


def _rms_norm(x, g):
    return x * lax.rsqrt(jnp.mean(x * x, -1, keepdims=True) + LN_EPS) * g


def _sigmoid(x):
    return 1.0 / (1.0 + jnp.exp(-x))


def _ln_in_kernel(xp_ref, xs_ref, g_ref, b_ref, o_ref, ob_ref, *, prompt_tiles):
    x = jnp.where(pl.program_id(0) < prompt_tiles, xp_ref[...], xs_ref[...])
    y = _layer_norm(x, g_ref[...], b_ref[...])
    o_ref[...] = y
    ob_ref[...] = y.astype(BF16)


def ln_in(x_prompt, x_sample, g, b, tm=256):
    D = x_prompt.shape[1]
    pt, st = x_prompt.shape[0] // tm, x_sample.shape[0] // tm
    T = (pt + st) * tm
    row = pl.BlockSpec((tm, D), lambda i: (i, 0))
    vec = pl.BlockSpec((1, D), lambda i: (0, 0))
    return pl.pallas_call(
        functools.partial(_ln_in_kernel, prompt_tiles=pt), grid=(pt + st,),
        in_specs=[pl.BlockSpec((tm, D), lambda i: (jnp.minimum(i, pt - 1), 0)),
                  pl.BlockSpec((tm, D), lambda i: (jnp.clip(i - pt, 0, st - 1), 0)), vec, vec],
        out_specs=[row, row],
        out_shape=[jax.ShapeDtypeStruct((T, D), F32), jax.ShapeDtypeStruct((T, D), BF16)],
        compiler_params=_cparams("arbitrary"), name="ln_in")(x_prompt, x_sample, g.reshape(1, D), b.reshape(1, D))


def _matmul_kernel(x_ref, w_ref, o_ref):
    o_ref[...] = _dot(x_ref[...], w_ref[...])


def matmul(x, w, tm=1024, tn=512):
    M, K = x.shape
    N = w.shape[1]
    return pl.pallas_call(
        _matmul_kernel, grid=(M // tm, N // tn),
        in_specs=[pl.BlockSpec((tm, K), lambda i, j: (i, 0)), pl.BlockSpec((K, tn), lambda i, j: (0, j))],
        out_specs=pl.BlockSpec((tm, tn), lambda i, j: (i, j)),
        out_shape=jax.ShapeDtypeStruct((M, N), F32),
        compiler_params=_cparams("parallel", "arbitrary"), name="w_in_proj")(x, w)


def _suffix_sum_matrix(n, width):
    r = lax.broadcasted_iota(jnp.int32, (2 * n, 2 * width), 0)
    c = lax.broadcasted_iota(jnp.int32, (2 * n, 2 * width), 1)
    j = jnp.where(r >= n, r - n, r)
    later = jnp.logical_or(c < width, jnp.logical_and(c < width + n, j > c - width))
    return jnp.where(later, 1.0, 0.0).astype(BF16)


def _sb_weights(z, run, sums, n_vis, diag):
    rows, nk = z.shape
    bw = min(nk, ATT_BLK)
    nb = nk // bw
    blk = lambda a, b: a[:, b * bw:(b + 1) * bw]
    zs = [blk(z, b) for b in range(n_vis)]
    sps = [jnp.maximum(zb, 0.0) + jnp.log2(1.0 + jnp.exp2(-jnp.abs(zb))) for zb in zs]
    if diag:
        visible = (lax.broadcasted_iota(jnp.int32, (rows, bw), 1) < lax.broadcasted_iota(jnp.int32, (rows, bw), 0))
        sps[-1] = jnp.where(visible, sps[-1], 0.0)
    stacked = jnp.concatenate([jnp.concatenate(_split_bf16(s), axis=-1) for s in sps], axis=0)
    sums_all = _dot(stacked, sums)
    sums_of = [sums_all[b * rows:(b + 1) * rows] for b in range(n_vis)]
    ws = [jnp.zeros((rows, bw), BF16)] * nb
    for b in reversed(range(n_vis)):
        later = sums_of[b][:, LANES:LANES + bw]
        w = jnp.exp2(zs[b] - sps[b] - later - run[:, :bw])
        if diag and b == n_vis - 1:
            w = jnp.where(visible, w, 0.0)
        ws[b] = w.astype(BF16)
        run = run + sums_of[b][:, :LANES]
    return jnp.concatenate(ws, axis=-1), run


def _attn_prompt_kernel(q_ref, k_ref, v_ref, o_ref, kb_ref, vb_ref, acc_ref, run_ref, z_ref, w_ref):
    qi = pl.program_id(2)
    nsub = ATT_TQ // ATT_BLK

    @pl.when(qi == 0)
    def _():
        kb_ref[...] = k_ref[...].astype(BF16)
        vb_ref[...] = v_ref[...].astype(BF16)

    sums = _suffix_sum_matrix(ATT_BLK, LANES)
    qs = (q_ref[...] * (HEAD_DIM ** -0.5 * LOG2_E)).astype(BF16)
    sub = lambda s: slice(s * ATT_BLK, (s + 1) * ATT_BLK)

    def scores(chunk, s):
        start = pl.multiple_of(chunk * ATT_TQ, ATT_TQ)
        z_ref[s] = _dot_nt(qs[sub(s)], kb_ref[pl.ds(start, ATT_TQ), :])

    def values(chunk, s):
        start = pl.multiple_of(chunk * ATT_TQ, ATT_TQ)
        return _dot(w_ref[s], vb_ref[pl.ds(start, ATT_TQ), :])

    for s in range(nsub):
        scores(qi, s)

    for s in range(nsub):
        z = z_ref[s]
        scores(jnp.maximum(qi - 1, 0), s)
        w_ref[s], run_ref[sub(s), :] = _sb_weights(z, jnp.zeros((ATT_BLK, LANES), F32), sums, s + 1, True)
    acc_ref[...] = jnp.zeros_like(acc_ref)

    def body(it, c):
        for s in range(nsub):
            z = z_ref[s]
            scores(jnp.maximum(qi - 2 - it, 0), s)
            pv = values(qi - it, s)
            w_ref[s], run_ref[sub(s), :] = _sb_weights(z, run_ref[sub(s), :], sums, nsub, False)
            acc_ref[sub(s), :] += pv
        return c

    lax.fori_loop(0, qi, body, 0)
    for s in range(nsub):
        o_ref[sub(s), :] = acc_ref[sub(s), :] + values(0, s)


def attn_prompt(proj, batch, seq):
    nq = seq // ATT_TQ
    q_spec = pl.BlockSpec((ATT_TQ, HEAD_DIM), lambda b, h, i: (b * nq + i, h))
    k_spec = pl.BlockSpec((seq, HEAD_DIM), lambda b, h, i: (b, N_HEADS + h))
    v_spec = pl.BlockSpec((seq, HEAD_DIM), lambda b, h, i: (b, 2 * N_HEADS + h))
    return pl.pallas_call(
        _attn_prompt_kernel, grid=(batch, N_HEADS, nq), in_specs=[q_spec, k_spec, v_spec],
        out_specs=q_spec, out_shape=jax.ShapeDtypeStruct((batch * seq, W_ATTN), F32),
        scratch_shapes=[pltpu.VMEM((seq, HEAD_DIM), BF16), pltpu.VMEM((seq, HEAD_DIM), BF16),
                        pltpu.VMEM((ATT_TQ, HEAD_DIM), F32), pltpu.VMEM((ATT_TQ, LANES), F32),
                        pltpu.VMEM((ATT_TQ // ATT_BLK, ATT_BLK, ATT_TQ), F32),
                        pltpu.VMEM((ATT_TQ // ATT_BLK, ATT_BLK, ATT_TQ), BF16)],
        compiler_params=_cparams("parallel", "parallel", "arbitrary"), name="attn_prompt")(proj, proj, proj)


def _attn_sample_kernel(q_ref, k_ref, v_ref, ck_ref, cv_ref, o_ref, qs_ref, kn_ref, vn_ref, oh_ref, *, past, tq):
    for h in range(N_HEADS):
        cols = slice(h * HEAD_DIM, (h + 1) * HEAD_DIM)
        qs_ref[h] = (q_ref[:, cols] * (HEAD_DIM ** -0.5 * LOG2_E)).astype(BF16)
        kn_ref[h] = k_ref[:, cols].astype(BF16)
        vn_ref[h] = v_ref[:, cols].astype(BF16)
    sums_new = _suffix_sum_matrix(tq, LANES)
    sums = _suffix_sum_matrix(ATT_BLK, LANES)

    def head(h):
        qs = qs_ref[h]
        w_new, run = _sb_weights(_dot_nt(qs, kn_ref[h]), jnp.zeros((tq, LANES), F32), sums_new, 1, True)
        cached = pl.ds(h, past, stride=N_HEADS)
        z = _dot_nt(qs, ck_ref[cached, :].astype(BF16))
        w_old, _ = _sb_weights(z, run, sums, past // ATT_BLK, False)
        oh_ref[h] = _dot(w_new, vn_ref[h]) + _dot(w_old, cv_ref[cached, :].astype(BF16))

    def head_pair(i, c):
        head(2 * i)
        head(2 * i + 1)
        return c

    lax.fori_loop(0, N_HEADS // 2, head_pair, 0)
    for h in range(N_HEADS):
        o_ref[:, h * HEAD_DIM:(h + 1) * HEAD_DIM] = oh_ref[h]


def attn_sample(proj, cache_k, cache_v, layer, row0, batch, tq):
    past = cache_k.shape[1] // N_HEADS
    rb0 = row0 // tq
    new = lambda c: pl.BlockSpec((tq, W_ATTN), lambda b: (rb0 + b, c))
    cache = pl.BlockSpec((None, past * N_HEADS, HEAD_DIM), lambda b: (layer * batch + b, 0, 0))
    per_head = lambda dt: pltpu.VMEM((N_HEADS, tq, HEAD_DIM), dt)
    return pl.pallas_call(
        functools.partial(_attn_sample_kernel, past=past, tq=tq), grid=(batch,),
        in_specs=[new(0), new(1), new(2), cache, cache],
        out_specs=pl.BlockSpec((tq, W_ATTN), lambda b: (b, 0)),
        out_shape=jax.ShapeDtypeStruct((batch * tq, W_ATTN), F32),
        scratch_shapes=[per_head(BF16), per_head(BF16), per_head(BF16), per_head(F32)],
        compiler_params=_cparams("parallel"), name="attn_sample")(proj, proj, proj, cache_k, cache_v)


def _ssm_param_kernel(lr_ref, li_ref, ls_ref, br_ref, bi_ref, ar_ref, ai_ref, bbr_ref, bbi_ref):
    lam_re = jnp.minimum(lr_ref[0], -1e-4)
    lam_im = li_ref[0]
    step = jnp.exp(ls_ref[0])
    mag = jnp.exp(lam_re * step)
    a_re = mag * jnp.cos(lam_im * step)
    a_im = mag * jnp.sin(lam_im * step)
    ar_ref[0] = a_re
    ai_ref[0] = a_im
    den = lam_re * lam_re + lam_im * lam_im
    nr = a_re - 1.0
    c_re = (nr * lam_re + a_im * lam_im) / den
    c_im = (a_im * lam_re - nr * lam_im) / den
    for c in range(SSM_GROUP):
        b_re = br_ref[0, c]
        b_im = bi_ref[0, c]
        bbr_ref[0, c] = c_re * b_re - c_im * b_im
        bbi_ref[0, c] = c_re * b_im + c_im * b_re


def ssm_params(lam_re, lam_im, log_step, b_re, b_im):
    G, P, C = N_SSM_GROUPS, SSM_STATE, SSM_GROUP
    gp = pl.BlockSpec((1, G, P), lambda l: (l, 0, 0))
    cgp = pl.BlockSpec((1, C, G, P), lambda l: (l, 0, 0, 0))
    return pl.pallas_call(
        _ssm_param_kernel, grid=(DEPTH,),
        in_specs=[gp, gp, pl.BlockSpec((1, G, 1), lambda l: (l, 0, 0)), cgp, cgp],
        out_specs=[gp, gp, cgp, cgp],
        out_shape=[jax.ShapeDtypeStruct((DEPTH, G, P), F32)] * 2 + [jax.ShapeDtypeStruct((DEPTH, C, G, P), F32)] * 2,
        compiler_params=_cparams("parallel"), name="ssm_params")(
            lam_re, lam_im, log_step.reshape(DEPTH, G, 1),
            b_re.transpose(0, 3, 1, 2), b_im.transpose(0, 3, 1, 2))


def _gelu_tanh(y):
    return 0.5 * y * (1.0 + jnp.tanh(math.sqrt(2.0 / math.pi) * (y + 0.044715 * (y * y * y))))


def _ssm_kernel(*refs, n_seq, rows, steps, seq_len, whole):
    n_io = 1 if whole else n_seq
    u_refs = refs[:n_io]
    (h0r_ref, h0i_ref, ar_ref, ai_ref, wb_ref, wcr_ref, wci_ref, d_ref, wg_ref, bg_ref) = refs[n_io:n_io + 10]
    o_ref, sr_ref, si_ref, us_ref, hr_ref, hi_ref, zs_ref = refs[n_io + 10:]
    t = pl.program_id(0)
    piece_in = SSM_LANES // SSM_PIECES
    piece_ch = W_SSM // SSM_PIECES
    lane_blk = 4 * SUBLANES * LANES // rows

    def seq_rows(b):
        return pl.ds(pl.multiple_of(b * seq_len + t * steps, steps), steps)

    @pl.when(t == 0)
    def _():
        sr_ref[...] = h0r_ref[...]
        si_ref[...] = h0i_ref[...]
        if n_seq < rows:
            us_ref[...] = jnp.zeros_like(us_ref)

    for b in range(n_seq):
        ub = u_refs[0][seq_rows(b), :] if whole else u_refs[b][...]
        for k in range(SSM_PIECES):
            us_ref[k, pl.ds(b, steps, stride=rows), :] = ub[:, k * piece_ch:(k + 1) * piece_ch]

    for k in range(SSM_PIECES):
        bu = _dot(us_ref[k].astype(BF16), wb_ref[k])
        hr_ref[:, k * piece_in:(k + 1) * piece_in] = bu[:, :piece_in]
        hi_ref[:, k * piece_in:(k + 1) * piece_in] = bu[:, piece_in:]

    for lb in range(SSM_LANES // lane_blk):
        ln = slice(lb * lane_blk, (lb + 1) * lane_blk)
        a_re = jnp.broadcast_to(ar_ref[:, ln], (rows, lane_blk))
        a_im = jnp.broadcast_to(ai_ref[:, ln], (rows, lane_blk))

        def step(j, carry, ln=ln, a_re=a_re, a_im=a_im):
            h_re, h_im = carry
            r0 = pl.multiple_of(j * rows, rows)
            n_re = a_re * h_re - a_im * h_im + hr_ref[pl.ds(r0, rows), ln]
            n_im = a_re * h_im + a_im * h_re + hi_ref[pl.ds(r0, rows), ln]
            hr_ref[pl.ds(r0, rows), ln] = n_re
            hi_ref[pl.ds(r0, rows), ln] = n_im
            return n_re, n_im

        h_re, h_im = lax.fori_loop(0, steps, step, (sr_ref[:, ln], si_ref[:, ln]), unroll=4)
        sr_ref[:, ln] = h_re
        si_ref[:, ln] = h_im

    zs = []
    for k in range(SSM_PIECES):
        ln = slice(k * piece_in, (k + 1) * piece_in)
        y = _dot(hr_ref[:, ln].astype(BF16), wcr_ref[k]) - _dot(hi_ref[:, ln].astype(BF16), wci_ref[k])
        zs.append(_gelu_tanh(y + d_ref[:, k * piece_ch:(k + 1) * piece_ch] * us_ref[k]))
    z = jnp.concatenate(zs, axis=-1)
    out = z * _sigmoid(_dot(z.astype(BF16), wg_ref[...]) + bg_ref[...])
    for k in range(SSM_PIECES):
        zs_ref[k] = out[:, k * piece_ch:(k + 1) * piece_ch]

    for b in range(n_seq):
        ob = jnp.concatenate([zs_ref[k, pl.ds(b, steps, stride=rows), :] for k in range(SSM_PIECES)], axis=-1)
        if whole:
            o_ref[seq_rows(b), :] = ob
        else:
            o_ref[b] = ob


def ssm_mixer(proj, row0, n_seq, seq_len, h0_re, h0_im, a_re, a_im, wb, wc_re, wc_im, d, w_glu, b_glu, steps, whole):
    rows = h0_re.shape[0]
    tile = rows * steps
    u_col = proj.shape[1] // W_SSM - 1
    full = lambda *s: pl.BlockSpec(s, lambda t: (0,) * len(s))
    if whole:
        u_specs = [pl.BlockSpec((n_seq * seq_len, W_SSM), lambda t: (row0 // (n_seq * seq_len), u_col))]
        o_spec = full(n_seq * seq_len, W_SSM)
        o_shape = jax.ShapeDtypeStruct((n_seq * seq_len, W_SSM), F32)
    else:
        u_specs = [pl.BlockSpec((steps, W_SSM), lambda t, b=b: (b * (seq_len // steps) + t, u_col)) for b in range(n_seq)]
        o_spec = pl.BlockSpec((n_seq, steps, W_SSM), lambda t: (0, t, 0))
        o_shape = jax.ShapeDtypeStruct((n_seq, seq_len, W_SSM), F32)
    state = full(rows, SSM_LANES)
    out, s_re, s_im = pl.pallas_call(
        functools.partial(_ssm_kernel, n_seq=n_seq, rows=rows, steps=steps, seq_len=seq_len, whole=whole),
        grid=(seq_len // steps,),
        in_specs=u_specs + [state, state, full(1, SSM_LANES), full(1, SSM_LANES), full(*wb.shape),
                            full(*wc_re.shape), full(*wc_im.shape), full(1, W_SSM), full(W_SSM, W_SSM), full(1, W_SSM)],
        out_specs=[o_spec, state, state],
        out_shape=[o_shape] + [jax.ShapeDtypeStruct((rows, SSM_LANES), F32)] * 2,
        scratch_shapes=[pltpu.VMEM((SSM_PIECES, tile, LANES), F32), pltpu.VMEM((tile, SSM_LANES), F32),
                        pltpu.VMEM((tile, SSM_LANES), F32), pltpu.VMEM((SSM_PIECES, tile, LANES), F32)],
        compiler_params=_cparams("arbitrary"), name="ssm_mixer")(
            *([proj] * len(u_specs)), h0_re, h0_im, a_re, a_im, wb, wc_re, wc_im, d, w_glu, b_glu.reshape(1, W_SSM))
    return out.reshape(n_seq, seq_len, W_SSM), s_re, s_im


def _block_diag(w):
    k, g, a, b = w.shape
    eye = jnp.eye(g, dtype=w.dtype)
    return (w[:, :, :, None, :] * eye[None, :, None, :, None]).reshape(k, g * a, g * b)


def ssm_weights(bb_re, bb_im, c_re, c_im):
    gpp = N_SSM_GROUPS // SSM_PIECES
    to_in = lambda b: b.transpose(1, 0, 2).reshape(SSM_PIECES, gpp, SSM_GROUP, SSM_STATE)
    wb = jnp.concatenate([_block_diag(to_in(bb_re)), _block_diag(to_in(bb_im))], axis=-1)
    to_out = lambda c: c.transpose(0, 2, 1).reshape(SSM_PIECES, gpp, SSM_STATE, SSM_GROUP)
    return wb.astype(BF16), _block_diag(to_out(c_re)).astype(BF16), _block_diag(to_out(c_im)).astype(BF16)


def _store_token_tiles(ref, x):
    for c in range(TOK_TILE):
        ref[pl.ds(c, x.shape[0], stride=TOK_TILE), :] = x[:, c * LANES:(c + 1) * LANES]


def _load_token_tiles(ref, rows):
    return jnp.concatenate([ref[pl.ds(c, rows, stride=TOK_TILE), :] for c in range(TOK_TILE)], axis=-1)


def _mix_out_kernel(ap_ref, as_ref, sp_ref, ss_ref, x_ref, ga_ref, gs_ref, w_ref, g_ref, b_ref,
                    h_ref, hb_ref, ht_ref, *, prompt_tiles):
    is_prompt = pl.program_id(0) < prompt_tiles
    a = _rms_norm(jnp.where(is_prompt, ap_ref[...], as_ref[...]), ga_ref[...]).astype(BF16)
    s = _rms_norm(jnp.where(is_prompt, sp_ref[...], ss_ref[...]), gs_ref[...]).astype(BF16)
    y = _dot(a, w_ref[:W_ATTN, :]) + _dot(s, w_ref[W_ATTN:, :])
    h = _layer_norm(ALPHA * x_ref[...] + y, g_ref[...], b_ref[...])
    h_ref[...] = h
    hb_ref[...] = h.astype(BF16)
    _store_token_tiles(ht_ref, h)


def mix_out(attn_p, attn_s, ssm_p, ssm_s, x, g_attn, g_ssm, w_out, ln_g, ln_b, tm=256):
    T, D = x.shape
    pt = attn_p.shape[0] // tm
    st = attn_s.shape[0] // tm
    prompt = pl.BlockSpec((tm, W_ATTN), lambda i: (jnp.minimum(i, pt - 1), 0))
    sample = pl.BlockSpec((tm, W_ATTN), lambda i: (jnp.clip(i - pt, 0, st - 1), 0))
    row = pl.BlockSpec((tm, D), lambda i: (i, 0))
    vec = lambda n: pl.BlockSpec((1, n), lambda i: (0, 0))
    return pl.pallas_call(
        functools.partial(_mix_out_kernel, prompt_tiles=pt), grid=(T // tm,),
        in_specs=[prompt, sample, prompt, sample, row, vec(W_ATTN), vec(W_SSM),
                  pl.BlockSpec((D, D), lambda i: (0, 0)), vec(D), vec(D)],
        out_specs=[row, row, pl.BlockSpec((tm * TOK_TILE, LANES), lambda i: (i, 0))],
        out_shape=[jax.ShapeDtypeStruct((T, D), F32), jax.ShapeDtypeStruct((T, D), BF16),
                   jax.ShapeDtypeStruct((T * TOK_TILE, LANES), F32)],
        compiler_params=_cparams("arbitrary"), name="mix_out_ln1")(
            attn_p, attn_s, ssm_p, ssm_s, x, g_attn.reshape(1, -1), g_ssm.reshape(1, -1), w_out,
            ln_g.reshape(1, D), ln_b.reshape(1, D))


def _router_kernel(h_ref, w_ref, b_ref, o_ref):
    tm = h_ref.shape[0]
    hh, hl = _split_bf16(h_ref[...])
    wh, wl = _split_bf16(w_ref[...])
    logits = _dot(hh, wh) + _dot(hh, wl) + _dot(hl, wh) + b_ref[...]
    lane = lax.broadcasted_iota(jnp.int32, (tm, LANES), 1).astype(F32)
    neg = -jnp.inf
    big = float(LANES)

    def first_max(v):
        m = jnp.max(v, -1, keepdims=True)
        return m, jnp.min(jnp.where(v == m, lane, big), -1, keepdims=True)

    is_group = lane < N_EXPERT_GROUPS
    g_max, g_idx = first_max(jnp.where(is_group, logits, neg))
    pg_top = 1.0 / jnp.sum(jnp.where(is_group, jnp.exp(logits - g_max), 0.0), -1, keepdims=True)
    lo = N_EXPERT_GROUPS + EXPERTS_PER_GROUP * g_idx
    le = jnp.where(lane >= lo, jnp.where(lane < lo + EXPERTS_PER_GROUP, logits, neg), neg)
    m1, i1 = first_max(le)
    m2, i2 = first_max(jnp.where(lane == i1, neg, le))
    e2 = jnp.exp(m2 - m1)
    den = 1.0 + e2
    o_ref[...] = jnp.where(lane == 0, i1 - N_EXPERT_GROUPS,
                 jnp.where(lane == 1, i2 - N_EXPERT_GROUPS,
                 jnp.where(lane == 2, pg_top / den,
                 jnp.where(lane == 3, pg_top * e2 / den, 0.0))))


def router(h, w_cat, b_cat, tm=512):
    T, D = h.shape
    return pl.pallas_call(
        _router_kernel, grid=(T // tm,),
        in_specs=[pl.BlockSpec((tm, D), lambda i: (i, 0)), pl.BlockSpec((D, LANES), lambda i: (0, 0)),
                  pl.BlockSpec((1, LANES), lambda i: (0, 0))],
        out_specs=pl.BlockSpec((tm, LANES), lambda i: (i, 0)),
        out_shape=jax.ShapeDtypeStruct((T, LANES), F32),
        compiler_params=_cparams("parallel"), name="router")(h, w_cat, b_cat)


def dispatch_plan(expert_ids, n_blocks):
    flat = expert_ids.reshape(-1)
    A = flat.shape[0]
    onehot = (flat[:, None] == jnp.arange(N_EXPERTS, dtype=jnp.int32)[None, :]).reshape(A // LANES, LANES, N_EXPERTS)
    lower = (jnp.arange(LANES)[:, None] >= jnp.arange(LANES)[None, :]).astype(BF16)
    within = jnp.einsum('ij,cje->cie', lower, onehot.astype(BF16), preferred_element_type=F32)
    chunk_total = within[:, -1, :]
    chunk_start = jnp.cumsum(chunk_total, axis=0) - chunk_total
    rank = jnp.sum(jnp.where(onehot, within - 1.0 + chunk_start[:, None, :], 0.0), axis=-1).reshape(A).astype(jnp.int32)
    counts = (chunk_start[-1] + chunk_total[-1]).astype(jnp.int32)
    blocks = (counts + MOE_ROWS - 1) // MOE_ROWS
    blk_end = jnp.cumsum(blocks)
    pad_start = (blk_end - blocks) * MOE_ROWS
    pos = pad_start[flat] + rank
    slot_tok = jnp.zeros((n_blocks * MOE_ROWS,), jnp.int32).at[pos].set(jnp.arange(A, dtype=jnp.int32) // TOP_K)
    n_used = blk_end[-1]
    blk = jnp.minimum(jnp.arange(n_blocks, dtype=jnp.int32), n_used - 1)
    block_expert = jnp.minimum(jnp.searchsorted(blk_end, blk, side='right'), N_EXPERTS - 1).astype(jnp.int32)
    return slot_tok, pos.astype(jnp.int32), block_expert, n_used.reshape(1).astype(jnp.int32)


def _tile_copy(src_ref, dst_ref, src_tok, dst_tok, sem):
    return pltpu.make_async_copy(src_ref.at[pl.ds(src_tok * TOK_TILE, TOK_TILE)],
                                 dst_ref.at[pl.ds(dst_tok * TOK_TILE, TOK_TILE)], sem)


def _expert_kernel(tok_ref, be_ref, nu_ref, ht_ref, wg_ref, wu_ref, wd_ref, o_ref,
                   xbuf_ref, wgb_ref, wub_ref, wdb_ref, sem):
    i = pl.program_id(0)
    n_used = nu_ref[0]
    slot = lax.rem(i, 2)

    def start_gather(block, s):
        def body(r2, c):
            for prio in range(DMA_PRIORITIES):
                r = r2 * DMA_PRIORITIES + prio
                _tile_copy(ht_ref, xbuf_ref.at[s], tok_ref[block * MOE_ROWS + r], r, sem.at[s]).start(priority=prio)
            return c
        lax.fori_loop(0, MOE_ROWS // DMA_PRIORITIES, body, 0, unroll=4)

    def wait_gather(s):
        def body(r, c):
            _tile_copy(ht_ref, xbuf_ref.at[s], 0, r, sem.at[s]).wait()
            return c
        lax.fori_loop(0, MOE_ROWS, body, 0, unroll=8)

    @pl.when(i == 0)
    def _():
        start_gather(0, 0)

    @pl.when(i + 1 < n_used)
    def _():
        start_gather(i + 1, 1 - slot)

    @pl.when(jnp.logical_or(i == 0, be_ref[i] != be_ref[jnp.maximum(i - 1, 0)]))
    def _():
        wgb_ref[...] = wg_ref[0].astype(BF16)
        wub_ref[...] = wu_ref[0].astype(BF16)
        wdb_ref[...] = wd_ref[0].astype(BF16)

    @pl.when(i < n_used)
    def _():
        wait_gather(slot)
        x = _load_token_tiles(xbuf_ref.at[slot], MOE_ROWS).astype(BF16)
        g = _dot(x, wgb_ref[...])
        u = _dot(x, wub_ref[...])
        hid = (g * _sigmoid(g)) * u
        _store_token_tiles(o_ref, _dot(hid.astype(BF16), wdb_ref[...]))

    @pl.when(i >= n_used)
    def _():
        o_ref[...] = jnp.zeros_like(o_ref)


def expert_mlp(h_tiles, slot_tok, block_expert, n_used, w_gate, w_up, w_down, layer):
    nb = block_expert.shape[0]
    blk_rows = MOE_ROWS * TOK_TILE
    D = D_MODEL
    weight = lambda a, b: pl.BlockSpec((1, a, b), lambda i, tok, be, nu: (layer * N_EXPERTS + be[i], 0, 0))
    return pl.pallas_call(
        _expert_kernel,
        grid_spec=pltpu.PrefetchScalarGridSpec(
            num_scalar_prefetch=3, grid=(nb,),
            in_specs=[pl.BlockSpec(memory_space=pl.ANY), weight(D, D_EXPERT), weight(D, D_EXPERT), weight(D_EXPERT, D)],
            out_specs=pl.BlockSpec((blk_rows, LANES), lambda i, tok, be, nu: (i, 0)),
            scratch_shapes=[pltpu.VMEM((2, blk_rows, LANES), F32), pltpu.VMEM((D, D_EXPERT), BF16),
                            pltpu.VMEM((D, D_EXPERT), BF16), pltpu.VMEM((D_EXPERT, D), BF16),
                            pltpu.SemaphoreType.DMA((2,))]),
        out_shape=jax.ShapeDtypeStruct((nb * blk_rows, LANES), F32),
        compiler_params=_cparams("arbitrary"), name="moe_experts")(
            slot_tok, block_expert, n_used, h_tiles, w_gate, w_up, w_down)


def _layer_out_kernel(pos_ref, h_ref, hb_ref, pp_ref, ps_ref, r_ref, yb_ref, wg_ref, wp_ref, g_ref, b_ref,
                      o_ref, ob_ref, buf_ref, sem, *, prompt_tiles):
    tm = h_ref.shape[0]
    base = pl.program_id(0) * tm
    p = jnp.where(pl.program_id(0) < prompt_tiles, pp_ref[...], ps_ref[...])

    def start(r, c):
        for k in range(TOP_K):
            _tile_copy(yb_ref, buf_ref.at[k], pos_ref[(base + r) * TOP_K + k], r, sem).start(priority=k % DMA_PRIORITIES)
        return c

    def wait(r, c):
        for k in range(TOP_K):
            _tile_copy(yb_ref, buf_ref.at[k], 0, r, sem).wait()
        return c

    lax.fori_loop(0, tm, start, 0, unroll=8)
    ple = _sigmoid(_dot(hb_ref[...], wg_ref[...])) * _dot(p.astype(BF16), wp_ref[...])
    lax.fori_loop(0, tm, wait, 0, unroll=8)
    route = r_ref[...]
    moe = (route[:, 2:3] * _load_token_tiles(buf_ref.at[0], tm)
           + route[:, 3:4] * _load_token_tiles(buf_ref.at[1], tm))
    x = _layer_norm(ALPHA * h_ref[...] + moe + ple, g_ref[...], b_ref[...])
    o_ref[...] = x
    ob_ref[...] = x.astype(BF16)


def layer_out(h, hb, p_prompt, p_sample, layer, route, pos, yb, w_ple_gate, w_ple_proj, ln_g, ln_b, tm=256):
    T, D = h.shape
    depth = w_ple_gate.shape[0] // D
    pt = p_prompt.shape[0] // depth // tm
    st = p_sample.shape[0] // depth // tm
    row = lambda n: pl.BlockSpec((tm, n), lambda i, pos: (i, 0))
    full = lambda a, b: pl.BlockSpec((a, b), lambda i, pos: (0, 0))
    side_p = pl.BlockSpec((tm, PLE_DIM), lambda i, pos: (layer * pt + jnp.minimum(i, pt - 1), 0))
    side_s = pl.BlockSpec((tm, PLE_DIM), lambda i, pos: (layer * st + jnp.clip(i - pt, 0, st - 1), 0))
    return pl.pallas_call(
        functools.partial(_layer_out_kernel, prompt_tiles=pt),
        grid_spec=pltpu.PrefetchScalarGridSpec(
            num_scalar_prefetch=1, grid=(T // tm,),
            in_specs=[row(D), row(D), side_p, side_s, row(LANES), pl.BlockSpec(memory_space=pl.ANY),
                      pl.BlockSpec((D, D), lambda i, pos: (layer, 0)), pl.BlockSpec((PLE_DIM, D), lambda i, pos: (layer, 0)),
                      full(1, D), full(1, D)],
            out_specs=[row(D), row(D)],
            scratch_shapes=[pltpu.VMEM((TOP_K, tm * TOK_TILE, LANES), F32), pltpu.SemaphoreType.DMA(())]),
        out_shape=[jax.ShapeDtypeStruct((T, D), F32), jax.ShapeDtypeStruct((T, D), BF16)],
        compiler_params=_cparams("arbitrary"), name="moe_combine_ple_ln2")(
            pos, h, hb, p_prompt, p_sample, route, yb, w_ple_gate, w_ple_proj, ln_g.reshape(1, D), ln_b.reshape(1, D))


def kernel(x_prompt, x_sample, p_prompt, p_sample, cache_k, cache_v, state_ssm_re, state_ssm_im, ln_in_g, ln_in_b, w_in, ssm_lam_re, ssm_lam_im, ssm_log_step, ssm_b_re, ssm_b_im, ssm_c_re, ssm_c_im, ssm_d, ssm_w_glu, ssm_b_glu, g_attn, g_ssm, w_out, ln1_g, ln1_b, w_router_group, b_router_group, w_router_expert, b_router_expert, w_exp_gate, w_exp_up, w_exp_down, w_ple_gate, w_ple_proj, ln2_g, ln2_b):
    B, L, D = x_prompt.shape
    SB, SL, _ = x_sample.shape
    depth = w_in.shape[0]
    past = cache_k.shape[2]
    TP, TS = B * L, SB * SL
    T = TP + TS
    n_blocks = (T * TOP_K) // MOE_ROWS + N_EXPERTS

    x, xb = ln_in(x_prompt.reshape(TP, D), x_sample.reshape(TS, D), ln_in_g, ln_in_b)
    p_p = p_prompt.reshape(depth * TP, PLE_DIM)
    p_s = p_sample.reshape(depth * TS, PLE_DIM)

    w_in_b = w_in.astype(BF16)
    w_out_b = w_out.astype(BF16)
    w_glu_b = ssm_w_glu.astype(BF16)
    w_pg_b = w_ple_gate.astype(BF16).reshape(depth * D, D)
    w_pp_b = w_ple_proj.astype(BF16).reshape(depth * PLE_DIM, D)
    w_route = jnp.pad(jnp.concatenate([w_router_group, w_router_expert], axis=-1),
                      ((0, 0), (0, 0), (0, LANES - N_EXPERT_GROUPS - N_EXPERTS)))
    b_route = jnp.pad(jnp.concatenate([b_router_group, b_router_expert], axis=-1),
                      ((0, 0), (0, LANES - N_EXPERT_GROUPS - N_EXPERTS))).reshape(depth, 1, LANES)
    a_re, a_im, bb_re, bb_im = ssm_params(ssm_lam_re, ssm_lam_im, ssm_log_step, ssm_b_re, ssm_b_im)

    cache_k_all = cache_k.reshape(depth * SB, past * N_HEADS, HEAD_DIM)
    cache_v_all = cache_v.reshape(depth * SB, past * N_HEADS, HEAD_DIM)
    w_eg = w_exp_gate.reshape(depth * N_EXPERTS, D, D_EXPERT)
    w_eu = w_exp_up.reshape(depth * N_EXPERTS, D, D_EXPERT)
    w_ed = w_exp_down.reshape(depth * N_EXPERTS, D_EXPERT, D)
    zeros_state = jnp.zeros((SUBLANES, SSM_LANES), F32)
    outs = [[] for _ in range(8)]
    for l in range(depth):
        proj = matmul(xb, w_in_b[l])
        attn_p = attn_prompt(proj, B, L)
        attn_s = attn_sample(proj, cache_k_all, cache_v_all, l, TP, SB, SL)

        wb, wc_re, wc_im = ssm_weights(bb_re[l], bb_im[l], ssm_c_re[l], ssm_c_im[l])
        lam_r = a_re[l].reshape(1, SSM_LANES)
        lam_i = a_im[l].reshape(1, SSM_LANES)
        d_row = ssm_d[l].reshape(1, W_SSM)
        ssm_p, hr_p, hi_p = ssm_mixer(proj, 0, B, L, zeros_state, zeros_state, lam_r, lam_i, wb, wc_re, wc_im,
                                      d_row, w_glu_b[l], ssm_b_glu[l], steps=64, whole=False)
        ssm_s, hr_s, hi_s = ssm_mixer(proj, TP, SB, SL, state_ssm_re[l].reshape(SB, SSM_LANES),
                                      state_ssm_im[l].reshape(SB, SSM_LANES), lam_r, lam_i, wb, wc_re, wc_im,
                                      d_row, w_glu_b[l], ssm_b_glu[l], steps=8, whole=True)
        h, hb, h_tiles = mix_out(attn_p, attn_s, ssm_p.reshape(TP, W_SSM), ssm_s.reshape(TS, W_SSM), x,
                                 g_attn[l], g_ssm[l], w_out_b[l], ln1_g[l], ln1_b[l])
        route = router(h, w_route[l], b_route[l])
        slot_tok, pos, block_expert, n_used = dispatch_plan(route[:, :TOP_K].astype(jnp.int32), n_blocks)
        yb = expert_mlp(h_tiles, slot_tok, block_expert, n_used, w_eg, w_eu, w_ed, l)
        x, xb = layer_out(h, hb, p_p, p_s, l, route, pos, yb, w_pg_b, w_pp_b, ln2_g[l], ln2_b[l])

        kv = lambda rows, c, b, s: proj[rows, c * W_ATTN:(c + 1) * W_ATTN].reshape(b, s, N_HEADS, HEAD_DIM)
        G, P = N_SSM_GROUPS, SSM_STATE
        for lst, val in zip(outs, (kv(slice(0, TP), 1, B, L), kv(slice(0, TP), 2, B, L),
                                   hr_p[:B].reshape(B, G, P), hi_p[:B].reshape(B, G, P),
                                   kv(slice(TP, T), 1, SB, SL), kv(slice(TP, T), 2, SB, SL),
                                   hr_s.reshape(SB, G, P), hi_s.reshape(SB, G, P))):
            lst.append(val)

    return (x[:TP].reshape(B, L, D), x[TP:].reshape(SB, SL, D)) + tuple(jnp.stack(o) for o in outs)
```

```python
import functools
import math

import jax
import jax.numpy as jnp
from jax import lax
from jax.experimental import pallas as pl
from jax.experimental.pallas import tpu as pltpu

F32 = jnp.float32
BF16 = jnp.bfloat16

D_MODEL = 2048
DEPTH = 4
W_ATTN = 1024
W_SSM = 1024
HEAD_DIM = 128
N_HEADS = 8
SSM_GROUP = 16
N_SSM_GROUPS = 64
SSM_STATE = 64
N_EXPERT_GROUPS = 4
EXPERTS_PER_GROUP = 8
N_EXPERTS = 32
TOP_K = 2
D_EXPERT = 512
PLE_DIM = 256
ALPHA = (2 * DEPTH) ** 0.25
LN_EPS = 1e-5
LOG2_E = math.log2(math.e)

LANES = 128
SUBLANES = 8
VMEM_LIMIT = 56 * 1024 * 1024
SSM_LANES = N_SSM_GROUPS * SSM_STATE
SSM_PIECES = 8
ATT_BLK = 128
ATT_TQ = 512
ATT_UNROLL = 4
MOE_ROWS = 256
TOK_TILE = D_MODEL // LANES
DMA_PRIORITIES = 2


def _cparams(*sem):
    return pltpu.CompilerParams(dimension_semantics=sem, vmem_limit_bytes=VMEM_LIMIT)


def _dot(a, b):
    return jnp.dot(a, b, preferred_element_type=F32)


def _dot_nt(a, b):
    return lax.dot_general(a, b, (((1,), (1,)), ((), ())), preferred_element_type=F32)


def _split_bf16(x):
    hi = x.astype(BF16)
    lo = (x - hi.astype(F32)).astype(BF16)
    return hi, lo


def _layer_norm(x, g, b):
    mu = jnp.mean(x, -1, keepdims=True)
    xc = x - mu
    var = jnp.mean(xc * xc, -1, keepdims=True)
    return xc * lax.rsqrt(var + LN_EPS) * g + b


def _rms_norm(x, g):
    return x * lax.rsqrt(jnp.mean(x * x, -1, keepdims=True) + LN_EPS) * g


def _sigmoid(x):
    return 1.0 / (1.0 + jnp.exp(-x))


def _ln_in_kernel(xp_ref, xs_ref, g_ref, b_ref, o_ref, ob_ref, *, prompt_tiles):
    x = jnp.where(pl.program_id(0) < prompt_tiles, xp_ref[...], xs_ref[...])
    y = _layer_norm(x, g_ref[...], b_ref[...])
    o_ref[...] = y
    ob_ref[...] = y.astype(BF16)


def ln_in(x_prompt, x_sample, g, b, tm=256):
    D = x_prompt.shape[1]
    pt, st = x_prompt.shape[0] // tm, x_sample.shape[0] // tm
    T = (pt + st) * tm
    row = pl.BlockSpec((tm, D), lambda i: (i, 0))
    vec = pl.BlockSpec((1, D), lambda i: (0, 0))
    return pl.pallas_call(
        functools.partial(_ln_in_kernel, prompt_tiles=pt), grid=(pt + st,),
        in_specs=[pl.BlockSpec((tm, D), lambda i: (jnp.minimum(i, pt - 1), 0)),
                  pl.BlockSpec((tm, D), lambda i: (jnp.clip(i - pt, 0, st - 1), 0)), vec, vec],
        out_specs=[row, row],
        out_shape=[jax.ShapeDtypeStruct((T, D), F32), jax.ShapeDtypeStruct((T, D), BF16)],
        compiler_params=_cparams("arbitrary"), name="ln_in")(x_prompt, x_sample, g.reshape(1, D), b.reshape(1, D))


def _w_in_kernel(x_ref, w_ref, kp_in, vp_in, ks_in, vs_in, o_ref, kp_ref, vp_ref, ks_ref, vs_ref, *, prompt_tiles):
    del kp_in, vp_in, ks_in, vs_in
    i, j = pl.program_id(0), pl.program_id(1)
    y = _dot(x_ref[...], w_ref[...])
    o_ref[...] = y

    def store_heads(ref):
        for h in range(N_HEADS):
            ref[pl.ds(h, y.shape[0], stride=N_HEADS), :] = y[:, h * HEAD_DIM:(h + 1) * HEAD_DIM]

    for col, p_ref, s_ref in ((1, kp_ref, ks_ref), (2, vp_ref, vs_ref)):
        @pl.when(jnp.logical_and(j == col, i < prompt_tiles))
        def _(p_ref=p_ref):
            store_heads(p_ref)

        @pl.when(jnp.logical_and(j == col, i >= prompt_tiles))
        def _(s_ref=s_ref):
            store_heads(s_ref)


def w_in_proj(x, w, layer, kv_bufs, prompt_rows, tm=512):
    M, K = x.shape
    tn = W_ATTN
    pt = prompt_rows // tm
    st = (M - prompt_rows) // tm
    heads = lambda idx: pl.BlockSpec((tm * N_HEADS, HEAD_DIM), idx)
    p_spec = heads(lambda i, j: (layer * pt + jnp.minimum(i, pt - 1), 0))
    s_spec = heads(lambda i, j: (layer * st + jnp.clip(i - pt, 0, st - 1), 0))
    any_spec = pl.BlockSpec(memory_space=pl.ANY)
    return pl.pallas_call(
        functools.partial(_w_in_kernel, prompt_tiles=pt), grid=(M // tm, w.shape[1] // tn),
        in_specs=[pl.BlockSpec((tm, K), lambda i, j: (i, 0)), pl.BlockSpec((K, tn), lambda i, j: (0, j)),
                  any_spec, any_spec, any_spec, any_spec],
        out_specs=[pl.BlockSpec((tm, tn), lambda i, j: (i, j)), p_spec, p_spec, s_spec, s_spec],
        out_shape=[jax.ShapeDtypeStruct((M, w.shape[1]), F32)] + [jax.ShapeDtypeStruct(b.shape, F32) for b in kv_bufs],
        input_output_aliases={2: 1, 3: 2, 4: 3, 5: 4},
        compiler_params=_cparams("arbitrary", "arbitrary"), name="w_in_proj")(x, w, *kv_bufs)


def _suffix_sum_matrix(n, width):
    r = lax.broadcasted_iota(jnp.int32, (2 * n, 2 * width), 0)
    c = lax.broadcasted_iota(jnp.int32, (2 * n, 2 * width), 1)
    j = jnp.where(r >= n, r - n, r)
    later = jnp.logical_or(c < width, jnp.logical_and(c < width + n, j > c - width))
    return jnp.where(later, 1.0, 0.0).astype(BF16)


def _sb_weights(z, run, sums, n_vis, diag):
    rows, nk = z.shape
    bw = min(nk, ATT_BLK)
    nb = nk // bw
    blk = lambda a, b: a[:, b * bw:(b + 1) * bw]
    zs = [blk(z, b) for b in range(n_vis)]
    sps = [jnp.maximum(zb, 0.0) + jnp.log2(1.0 + jnp.exp2(-jnp.abs(zb))) for zb in zs]
    if diag:
        visible = (lax.broadcasted_iota(jnp.int32, (rows, bw), 1) < lax.broadcasted_iota(jnp.int32, (rows, bw), 0))
        sps[-1] = jnp.where(visible, sps[-1], 0.0)
    stacked = jnp.concatenate([jnp.concatenate(_split_bf16(s), axis=-1) for s in sps], axis=0)
    sums_all = _dot(stacked, sums)
    sums_of = [sums_all[b * rows:(b + 1) * rows] for b in range(n_vis)]
    ws = [jnp.zeros((rows, bw), BF16)] * nb
    for b in reversed(range(n_vis)):
        later = sums_of[b][:, LANES:LANES + bw]
        w = jnp.exp2(zs[b] - sps[b] - later - run[:, :bw])
        if diag and b == n_vis - 1:
            w = jnp.where(visible, w, 0.0)
        ws[b] = w.astype(BF16)
        run = run + sums_of[b][:, :LANES]
    return jnp.concatenate(ws, axis=-1), run


def _attn_prompt_kernel(q_ref, k_ref, v_ref, o_ref, kb_ref, vb_ref, acc_ref, run_ref, z_ref, w_ref):
    qi = pl.program_id(2)
    nsub = ATT_TQ // ATT_BLK

    @pl.when(qi == 0)
    def _():
        kb_ref[...] = k_ref[...].astype(BF16)
        vb_ref[...] = v_ref[...].astype(BF16)

    sums = _suffix_sum_matrix(ATT_BLK, LANES)
    qs = (q_ref[...] * (HEAD_DIM ** -0.5 * LOG2_E)).astype(BF16)
    sub = lambda s: slice(s * ATT_BLK, (s + 1) * ATT_BLK)

    def scores(chunk, s):
        start = pl.multiple_of(chunk * ATT_TQ, ATT_TQ)
        z_ref[s] = _dot_nt(qs[sub(s)], kb_ref[pl.ds(start, ATT_TQ), :])

    def values(chunk, s):
        start = pl.multiple_of(chunk * ATT_TQ, ATT_TQ)
        return _dot(w_ref[s], vb_ref[pl.ds(start, ATT_TQ), :])

    for s in range(nsub):
        scores(qi, s)

    for s in range(nsub):
        z = z_ref[s]
        scores(jnp.maximum(qi - 1, 0), s)
        w_ref[s], run_ref[sub(s), :] = _sb_weights(z, jnp.zeros((ATT_BLK, LANES), F32), sums, s + 1, True)
    acc_ref[...] = jnp.zeros_like(acc_ref)

    def body(it, c):
        for s in range(nsub):
            z = z_ref[s]
            scores(jnp.maximum(qi - 2 - it, 0), s)
            pv = values(qi - it, s)
            w_ref[s], run_ref[sub(s), :] = _sb_weights(z, run_ref[sub(s), :], sums, nsub, False)
            acc_ref[sub(s), :] += pv
        return c

    lax.fori_loop(0, qi, body, 0)
    for s in range(nsub):
        o_ref[sub(s), :] = acc_ref[sub(s), :] + values(0, s)


def attn_prompt(proj, batch, seq):
    nq = seq // ATT_TQ
    q_spec = pl.BlockSpec((ATT_TQ, HEAD_DIM), lambda b, h, i: (b * nq + i, h))
    k_spec = pl.BlockSpec((seq, HEAD_DIM), lambda b, h, i: (b, N_HEADS + h))
    v_spec = pl.BlockSpec((seq, HEAD_DIM), lambda b, h, i: (b, 2 * N_HEADS + h))
    return pl.pallas_call(
        _attn_prompt_kernel, grid=(batch, N_HEADS, nq), in_specs=[q_spec, k_spec, v_spec],
        out_specs=q_spec, out_shape=jax.ShapeDtypeStruct((batch * seq, W_ATTN), F32),
        scratch_shapes=[pltpu.VMEM((seq, HEAD_DIM), BF16), pltpu.VMEM((seq, HEAD_DIM), BF16),
                        pltpu.VMEM((ATT_TQ, HEAD_DIM), F32), pltpu.VMEM((ATT_TQ, LANES), F32),
                        pltpu.VMEM((ATT_TQ // ATT_BLK, ATT_BLK, ATT_TQ), F32),
                        pltpu.VMEM((ATT_TQ // ATT_BLK, ATT_BLK, ATT_TQ), BF16)],
        compiler_params=_cparams("parallel", "parallel", "arbitrary"), name="attn_prompt")(proj, proj, proj)


def _attn_sample_kernel(q_ref, k_ref, v_ref, ck_ref, cv_ref, o_ref, qs_ref, kn_ref, vn_ref, oh_ref, *, past, tq):
    for h in range(N_HEADS):
        cols = slice(h * HEAD_DIM, (h + 1) * HEAD_DIM)
        qs_ref[h] = (q_ref[:, cols] * (HEAD_DIM ** -0.5 * LOG2_E)).astype(BF16)
        kn_ref[h] = k_ref[:, cols].astype(BF16)
        vn_ref[h] = v_ref[:, cols].astype(BF16)
    sums_new = _suffix_sum_matrix(tq, LANES)
    sums = _suffix_sum_matrix(ATT_BLK, LANES)

    def head(h):
        qs = qs_ref[h]
        w_new, run = _sb_weights(_dot_nt(qs, kn_ref[h]), jnp.zeros((tq, LANES), F32), sums_new, 1, True)
        cached = pl.ds(h, past, stride=N_HEADS)
        z = _dot_nt(qs, ck_ref[cached, :].astype(BF16))
        w_old, _ = _sb_weights(z, run, sums, past // ATT_BLK, False)
        oh_ref[h] = _dot(w_new, vn_ref[h]) + _dot(w_old, cv_ref[cached, :].astype(BF16))

    def head_pair(i, c):
        head(2 * i)
        head(2 * i + 1)
        return c

    lax.fori_loop(0, N_HEADS // 2, head_pair, 0)
    for h in range(N_HEADS):
        o_ref[:, h * HEAD_DIM:(h + 1) * HEAD_DIM] = oh_ref[h]


def attn_sample(proj, cache_k, cache_v, layer, row0, batch, tq):
    past = cache_k.shape[1] // N_HEADS
    rb0 = row0 // tq
    new = lambda c: pl.BlockSpec((tq, W_ATTN), lambda b: (rb0 + b, c))
    cache = pl.BlockSpec((None, past * N_HEADS, HEAD_DIM), lambda b: (layer * batch + b, 0, 0))
    per_head = lambda dt: pltpu.VMEM((N_HEADS, tq, HEAD_DIM), dt)
    return pl.pallas_call(
        functools.partial(_attn_sample_kernel, past=past, tq=tq), grid=(batch,),
        in_specs=[new(0), new(1), new(2), cache, cache],
        out_specs=pl.BlockSpec((tq, W_ATTN), lambda b: (b, 0)),
        out_shape=jax.ShapeDtypeStruct((batch * tq, W_ATTN), F32),
        scratch_shapes=[per_head(BF16), per_head(BF16), per_head(BF16), per_head(F32)],
        compiler_params=_cparams("parallel"), name="attn_sample")(proj, proj, proj, cache_k, cache_v)


def _ssm_param_kernel(lr_ref, li_ref, ls_ref, br_ref, bi_ref, ar_ref, ai_ref, bbr_ref, bbi_ref):
    lam_re = jnp.minimum(lr_ref[0], -1e-4)
    lam_im = li_ref[0]
    step = jnp.exp(ls_ref[0])
    mag = jnp.exp(lam_re * step)
    a_re = mag * jnp.cos(lam_im * step)
    a_im = mag * jnp.sin(lam_im * step)
    ar_ref[0] = a_re
    ai_ref[0] = a_im
    den = lam_re * lam_re + lam_im * lam_im
    nr = a_re - 1.0
    c_re = (nr * lam_re + a_im * lam_im) / den
    c_im = (a_im * lam_re - nr * lam_im) / den
    for c in range(SSM_GROUP):
        b_re = br_ref[0, c]
        b_im = bi_ref[0, c]
        bbr_ref[0, c] = c_re * b_re - c_im * b_im
        bbi_ref[0, c] = c_re * b_im + c_im * b_re


def ssm_params(lam_re, lam_im, log_step, b_re, b_im):
    G, P, C = N_SSM_GROUPS, SSM_STATE, SSM_GROUP
    gp = pl.BlockSpec((1, G, P), lambda l: (l, 0, 0))
    cgp = pl.BlockSpec((1, C, G, P), lambda l: (l, 0, 0, 0))
    return pl.pallas_call(
        _ssm_param_kernel, grid=(DEPTH,),
        in_specs=[gp, gp, pl.BlockSpec((1, G, 1), lambda l: (l, 0, 0)), cgp, cgp],
        out_specs=[gp, gp, cgp, cgp],
        out_shape=[jax.ShapeDtypeStruct((DEPTH, G, P), F32)] * 2 + [jax.ShapeDtypeStruct((DEPTH, C, G, P), F32)] * 2,
        compiler_params=_cparams("parallel"), name="ssm_params")(
            lam_re, lam_im, log_step.reshape(DEPTH, G, 1),
            b_re.transpose(0, 3, 1, 2), b_im.transpose(0, 3, 1, 2))


def _gelu_tanh(y):
    return 0.5 * y * (1.0 + jnp.tanh(math.sqrt(2.0 / math.pi) * (y + 0.044715 * (y * y * y))))


def _ssm_kernel(*refs, n_seq, rows, steps, seq_len, whole):
    n_io = 1 if whole else n_seq
    u_refs = refs[:n_io]
    (h0r_ref, h0i_ref, ar_ref, ai_ref, wb_ref, wcr_ref, wci_ref, d_ref, wg_ref, bg_ref) = refs[n_io:n_io + 10]
    o_ref, sr_ref, si_ref, us_ref, hr_ref, hi_ref, zs_ref = refs[n_io + 10:]
    t = pl.program_id(0)
    piece_in = SSM_LANES // SSM_PIECES
    piece_ch = W_SSM // SSM_PIECES
    lane_blk = 4 * SUBLANES * LANES // rows

    def seq_rows(b):
        return pl.ds(pl.multiple_of(b * seq_len + t * steps, steps), steps)

    @pl.when(t == 0)
    def _():
        sr_ref[...] = h0r_ref[...]
        si_ref[...] = h0i_ref[...]
        if n_seq < rows:
            us_ref[...] = jnp.zeros_like(us_ref)

    for b in range(n_seq):
        ub = u_refs[0][seq_rows(b), :] if whole else u_refs[b][...]
        for k in range(SSM_PIECES):
            us_ref[k, pl.ds(b, steps, stride=rows), :] = ub[:, k * piece_ch:(k + 1) * piece_ch]

    for k in range(SSM_PIECES):
        bu = _dot(us_ref[k].astype(BF16), wb_ref[k])
        hr_ref[:, k * piece_in:(k + 1) * piece_in] = bu[:, :piece_in]
        hi_ref[:, k * piece_in:(k + 1) * piece_in] = bu[:, piece_in:]

    for lb in range(SSM_LANES // lane_blk):
        ln = slice(lb * lane_blk, (lb + 1) * lane_blk)
        a_re = jnp.broadcast_to(ar_ref[:, ln], (rows, lane_blk))
        a_im = jnp.broadcast_to(ai_ref[:, ln], (rows, lane_blk))

        def step(j, carry, ln=ln, a_re=a_re, a_im=a_im):
            h_re, h_im = carry
            r0 = pl.multiple_of(j * rows, rows)
            n_re = a_re * h_re - a_im * h_im + hr_ref[pl.ds(r0, rows), ln]
            n_im = a_re * h_im + a_im * h_re + hi_ref[pl.ds(r0, rows), ln]
            hr_ref[pl.ds(r0, rows), ln] = n_re
            hi_ref[pl.ds(r0, rows), ln] = n_im
            return n_re, n_im

        h_re, h_im = lax.fori_loop(0, steps, step, (sr_ref[:, ln], si_ref[:, ln]), unroll=4)
        sr_ref[:, ln] = h_re
        si_ref[:, ln] = h_im

    zs = []
    for k in range(SSM_PIECES):
        ln = slice(k * piece_in, (k + 1) * piece_in)
        y = _dot(hr_ref[:, ln].astype(BF16), wcr_ref[k]) - _dot(hi_ref[:, ln].astype(BF16), wci_ref[k])
        zs.append(_gelu_tanh(y + d_ref[:, k * piece_ch:(k + 1) * piece_ch] * us_ref[k]))
    z = jnp.concatenate(zs, axis=-1)
    out = z * _sigmoid(_dot(z.astype(BF16), wg_ref[...]) + bg_ref[...])
    for k in range(SSM_PIECES):
        zs_ref[k] = out[:, k * piece_ch:(k + 1) * piece_ch]

    for b in range(n_seq):
        ob = jnp.concatenate([zs_ref[k, pl.ds(b, steps, stride=rows), :] for k in range(SSM_PIECES)], axis=-1)
        if whole:
            o_ref[seq_rows(b), :] = ob
        else:
            o_ref[b] = ob


def ssm_mixer(proj, row0, n_seq, seq_len, h0_re, h0_im, a_re, a_im, wb, wc_re, wc_im, d, w_glu, b_glu, steps, whole):
    rows = h0_re.shape[0]
    tile = rows * steps
    u_col = proj.shape[1] // W_SSM - 1
    full = lambda *s: pl.BlockSpec(s, lambda t: (0,) * len(s))
    if whole:
        u_specs = [pl.BlockSpec((n_seq * seq_len, W_SSM), lambda t: (row0 // (n_seq * seq_len), u_col))]
        o_spec = full(n_seq * seq_len, W_SSM)
        o_shape = jax.ShapeDtypeStruct((n_seq * seq_len, W_SSM), F32)
    else:
        u_specs = [pl.BlockSpec((steps, W_SSM), lambda t, b=b: (b * (seq_len // steps) + t, u_col)) for b in range(n_seq)]
        o_spec = pl.BlockSpec((n_seq, steps, W_SSM), lambda t: (0, t, 0))
        o_shape = jax.ShapeDtypeStruct((n_seq, seq_len, W_SSM), F32)
    state = full(rows, SSM_LANES)
    out, s_re, s_im = pl.pallas_call(
        functools.partial(_ssm_kernel, n_seq=n_seq, rows=rows, steps=steps, seq_len=seq_len, whole=whole),
        grid=(seq_len // steps,),
        in_specs=u_specs + [state, state, full(1, SSM_LANES), full(1, SSM_LANES), full(*wb.shape),
                            full(*wc_re.shape), full(*wc_im.shape), full(1, W_SSM), full(W_SSM, W_SSM), full(1, W_SSM)],
        out_specs=[o_spec, state, state],
        out_shape=[o_shape] + [jax.ShapeDtypeStruct((rows, SSM_LANES), F32)] * 2,
        scratch_shapes=[pltpu.VMEM((SSM_PIECES, tile, LANES), F32), pltpu.VMEM((tile, SSM_LANES), F32),
                        pltpu.VMEM((tile, SSM_LANES), F32), pltpu.VMEM((SSM_PIECES, tile, LANES), F32)],
        compiler_params=_cparams("arbitrary"), name="ssm_mixer")(
            *([proj] * len(u_specs)), h0_re, h0_im, a_re, a_im, wb, wc_re, wc_im, d, w_glu, b_glu.reshape(1, W_SSM))
    return out.reshape(n_seq, seq_len, W_SSM), s_re, s_im


def _block_diag(w):
    k, g, a, b = w.shape
    eye = jnp.eye(g, dtype=w.dtype)
    return (w[:, :, :, None, :] * eye[None, :, None, :, None]).reshape(k, g * a, g * b)


def ssm_weights(bb_re, bb_im, c_re, c_im):
    gpp = N_SSM_GROUPS // SSM_PIECES
    to_in = lambda b: b.transpose(1, 0, 2).reshape(SSM_PIECES, gpp, SSM_GROUP, SSM_STATE)
    wb = jnp.concatenate([_block_diag(to_in(bb_re)), _block_diag(to_in(bb_im))], axis=-1)
    to_out = lambda c: c.transpose(0, 2, 1).reshape(SSM_PIECES, gpp, SSM_STATE, SSM_GROUP)
    return wb.astype(BF16), _block_diag(to_out(c_re)).astype(BF16), _block_diag(to_out(c_im)).astype(BF16)


def _store_token_tiles(ref, x):
    for c in range(TOK_TILE):
        ref[pl.ds(c, x.shape[0], stride=TOK_TILE), :] = x[:, c * LANES:(c + 1) * LANES]


def _load_token_tiles(ref, rows):
    return jnp.concatenate([ref[pl.ds(c, rows, stride=TOK_TILE), :] for c in range(TOK_TILE)], axis=-1)


def _mix_out_kernel(ap_ref, as_ref, sp_ref, ss_ref, x_ref, ga_ref, gs_ref, w_ref, g_ref, b_ref,
                    h_ref, hb_ref, ht_ref, *, prompt_tiles):
    is_prompt = pl.program_id(0) < prompt_tiles
    a = _rms_norm(jnp.where(is_prompt, ap_ref[...], as_ref[...]), ga_ref[...]).astype(BF16)
    s = _rms_norm(jnp.where(is_prompt, sp_ref[...], ss_ref[...]), gs_ref[...]).astype(BF16)
    y = _dot(a, w_ref[:W_ATTN, :]) + _dot(s, w_ref[W_ATTN:, :])
    h = _layer_norm(ALPHA * x_ref[...] + y, g_ref[...], b_ref[...])
    h_ref[...] = h
    hb_ref[...] = h.astype(BF16)
    _store_token_tiles(ht_ref, h)


def mix_out(attn_p, attn_s, ssm_p, ssm_s, x, g_attn, g_ssm, w_out, ln_g, ln_b, tm=256):
    T, D = x.shape
    pt = attn_p.shape[0] // tm
    st = attn_s.shape[0] // tm
    prompt = pl.BlockSpec((tm, W_ATTN), lambda i: (jnp.minimum(i, pt - 1), 0))
    sample = pl.BlockSpec((tm, W_ATTN), lambda i: (jnp.clip(i - pt, 0, st - 1), 0))
    row = pl.BlockSpec((tm, D), lambda i: (i, 0))
    vec = lambda n: pl.BlockSpec((1, n), lambda i: (0, 0))
    return pl.pallas_call(
        functools.partial(_mix_out_kernel, prompt_tiles=pt), grid=(T // tm,),
        in_specs=[prompt, sample, prompt, sample, row, vec(W_ATTN), vec(W_SSM),
                  pl.BlockSpec((D, D), lambda i: (0, 0)), vec(D), vec(D)],
        out_specs=[row, row, pl.BlockSpec((tm * TOK_TILE, LANES), lambda i: (i, 0))],
        out_shape=[jax.ShapeDtypeStruct((T, D), F32), jax.ShapeDtypeStruct((T, D), BF16),
                   jax.ShapeDtypeStruct((T * TOK_TILE, LANES), F32)],
        compiler_params=_cparams("arbitrary"), name="mix_out_ln1")(
            attn_p, attn_s, ssm_p, ssm_s, x, g_attn.reshape(1, -1), g_ssm.reshape(1, -1), w_out,
            ln_g.reshape(1, D), ln_b.reshape(1, D))


def _router_kernel(h_ref, w_ref, b_ref, o_ref):
    tm = h_ref.shape[0]
    hh, hl = _split_bf16(h_ref[...])
    wh, wl = _split_bf16(w_ref[...])
    logits = _dot(hh, wh) + _dot(hh, wl) + _dot(hl, wh) + b_ref[...]
    lane = lax.broadcasted_iota(jnp.int32, (tm, LANES), 1).astype(F32)
    neg = -jnp.inf
    big = float(LANES)

    def first_max(v):
        m = jnp.max(v, -1, keepdims=True)
        return m, jnp.min(jnp.where(v == m, lane, big), -1, keepdims=True)

    is_group = lane < N_EXPERT_GROUPS
    g_max, g_idx = first_max(jnp.where(is_group, logits, neg))
    pg_top = 1.0 / jnp.sum(jnp.where(is_group, jnp.exp(logits - g_max), 0.0), -1, keepdims=True)
    lo = N_EXPERT_GROUPS + EXPERTS_PER_GROUP * g_idx
    le = jnp.where(lane >= lo, jnp.where(lane < lo + EXPERTS_PER_GROUP, logits, neg), neg)
    m1, i1 = first_max(le)
    m2, i2 = first_max(jnp.where(lane == i1, neg, le))
    e2 = jnp.exp(m2 - m1)
    den = 1.0 + e2
    o_ref[...] = jnp.where(lane == 0, i1 - N_EXPERT_GROUPS,
                 jnp.where(lane == 1, i2 - N_EXPERT_GROUPS,
                 jnp.where(lane == 2, pg_top / den,
                 jnp.where(lane == 3, pg_top * e2 / den, 0.0))))


def router(h, w_cat, b_cat, tm=512):
    T, D = h.shape
    return pl.pallas_call(
        _router_kernel, grid=(T // tm,),
        in_specs=[pl.BlockSpec((tm, D), lambda i: (i, 0)), pl.BlockSpec((D, LANES), lambda i: (0, 0)),
                  pl.BlockSpec((1, LANES), lambda i: (0, 0))],
        out_specs=pl.BlockSpec((tm, LANES), lambda i: (i, 0)),
        out_shape=jax.ShapeDtypeStruct((T, LANES), F32),
        compiler_params=_cparams("parallel"), name="router")(h, w_cat, b_cat)


def dispatch_plan(expert_ids, n_blocks):
    flat = expert_ids.reshape(-1)
    A = flat.shape[0]
    onehot = (flat[:, None] == jnp.arange(N_EXPERTS, dtype=jnp.int32)[None, :]).reshape(A // LANES, LANES, N_EXPERTS)
    lower = (jnp.arange(LANES)[:, None] >= jnp.arange(LANES)[None, :]).astype(BF16)
    within = jnp.einsum('ij,cje->cie', lower, onehot.astype(BF16), preferred_element_type=F32)
    chunk_total = within[:, -1, :]
    chunk_start = jnp.cumsum(chunk_total, axis=0) - chunk_total
    rank = jnp.sum(jnp.where(onehot, within - 1.0 + chunk_start[:, None, :], 0.0), axis=-1).reshape(A).astype(jnp.int32)
    counts = (chunk_start[-1] + chunk_total[-1]).astype(jnp.int32)
    blocks = (counts + MOE_ROWS - 1) // MOE_ROWS
    blk_end = jnp.cumsum(blocks)
    pad_start = (blk_end - blocks) * MOE_ROWS
    pos = pad_start[flat] + rank
    slot_tok = jnp.zeros((n_blocks * MOE_ROWS,), jnp.int32).at[pos].set(jnp.arange(A, dtype=jnp.int32) // TOP_K)
    n_used = blk_end[-1]
    blk = jnp.minimum(jnp.arange(n_blocks, dtype=jnp.int32), n_used - 1)
    block_expert = jnp.minimum(jnp.searchsorted(blk_end, blk, side='right'), N_EXPERTS - 1).astype(jnp.int32)
    return slot_tok, pos.astype(jnp.int32), block_expert, n_used.reshape(1).astype(jnp.int32)


def _tile_copy(src_ref, dst_ref, src_tok, dst_tok, sem):
    return pltpu.make_async_copy(src_ref.at[pl.ds(src_tok * TOK_TILE, TOK_TILE)],
                                 dst_ref.at[pl.ds(dst_tok * TOK_TILE, TOK_TILE)], sem)


def _expert_kernel(tok_ref, be_ref, nu_ref, ht_ref, wg_ref, wu_ref, wd_ref, o_ref,
                   xbuf_ref, wgb_ref, wub_ref, wdb_ref, sem):
    i = pl.program_id(0)
    n_used = nu_ref[0]
    slot = lax.rem(i, 2)

    def start_gather(block, s):
        def body(r2, c):
            for prio in range(DMA_PRIORITIES):
                r = r2 * DMA_PRIORITIES + prio
                _tile_copy(ht_ref, xbuf_ref.at[s], tok_ref[block * MOE_ROWS + r], r, sem.at[s]).start(priority=prio)
            return c
        lax.fori_loop(0, MOE_ROWS // DMA_PRIORITIES, body, 0, unroll=4)

    def wait_gather(s):
        def body(r, c):
            _tile_copy(ht_ref, xbuf_ref.at[s], 0, r, sem.at[s]).wait()
            return c
        lax.fori_loop(0, MOE_ROWS, body, 0, unroll=8)

    @pl.when(i == 0)
    def _():
        start_gather(0, 0)

    @pl.when(i + 1 < n_used)
    def _():
        start_gather(i + 1, 1 - slot)

    @pl.when(jnp.logical_or(i == 0, be_ref[i] != be_ref[jnp.maximum(i - 1, 0)]))
    def _():
        wgb_ref[...] = wg_ref[0].astype(BF16)
        wub_ref[...] = wu_ref[0].astype(BF16)
        wdb_ref[...] = wd_ref[0].astype(BF16)

    @pl.when(i < n_used)
    def _():
        wait_gather(slot)
        x = _load_token_tiles(xbuf_ref.at[slot], MOE_ROWS).astype(BF16)
        g = _dot(x, wgb_ref[...])
        u = _dot(x, wub_ref[...])
        hid = (g * _sigmoid(g)) * u
        _store_token_tiles(o_ref, _dot(hid.astype(BF16), wdb_ref[...]))

    @pl.when(i >= n_used)
    def _():
        o_ref[...] = jnp.zeros_like(o_ref)


def expert_mlp(h_tiles, slot_tok, block_expert, n_used, w_gate, w_up, w_down, layer):
    nb = block_expert.shape[0]
    blk_rows = MOE_ROWS * TOK_TILE
    D = D_MODEL
    weight = lambda a, b: pl.BlockSpec((1, a, b), lambda i, tok, be, nu: (layer * N_EXPERTS + be[i], 0, 0))
    return pl.pallas_call(
        _expert_kernel,
        grid_spec=pltpu.PrefetchScalarGridSpec(
            num_scalar_prefetch=3, grid=(nb,),
            in_specs=[pl.BlockSpec(memory_space=pl.ANY), weight(D, D_EXPERT), weight(D, D_EXPERT), weight(D_EXPERT, D)],
            out_specs=pl.BlockSpec((blk_rows, LANES), lambda i, tok, be, nu: (i, 0)),
            scratch_shapes=[pltpu.VMEM((2, blk_rows, LANES), F32), pltpu.VMEM((D, D_EXPERT), BF16),
                            pltpu.VMEM((D, D_EXPERT), BF16), pltpu.VMEM((D_EXPERT, D), BF16),
                            pltpu.SemaphoreType.DMA((2,))]),
        out_shape=jax.ShapeDtypeStruct((nb * blk_rows, LANES), F32),
        compiler_params=_cparams("arbitrary"), name="moe_experts")(
            slot_tok, block_expert, n_used, h_tiles, w_gate, w_up, w_down)


def _layer_out_kernel(pos_ref, h_ref, hb_ref, pp_ref, ps_ref, r_ref, yb_ref, wg_ref, wp_ref, g_ref, b_ref,
                      o_ref, ob_ref, buf_ref, sem, *, prompt_tiles):
    tm = h_ref.shape[0]
    base = pl.program_id(0) * tm
    p = jnp.where(pl.program_id(0) < prompt_tiles, pp_ref[...], ps_ref[...])

    def start(r, c):
        for k in range(TOP_K):
            _tile_copy(yb_ref, buf_ref.at[k], pos_ref[(base + r) * TOP_K + k], r, sem).start(priority=k % DMA_PRIORITIES)
        return c

    def wait(r, c):
        for k in range(TOP_K):
            _tile_copy(yb_ref, buf_ref.at[k], 0, r, sem).wait()
        return c

    lax.fori_loop(0, tm, start, 0, unroll=8)
    ple = _sigmoid(_dot(hb_ref[...], wg_ref[...])) * _dot(p.astype(BF16), wp_ref[...])
    lax.fori_loop(0, tm, wait, 0, unroll=8)
    route = r_ref[...]
    moe = (route[:, 2:3] * _load_token_tiles(buf_ref.at[0], tm)
           + route[:, 3:4] * _load_token_tiles(buf_ref.at[1], tm))
    x = _layer_norm(ALPHA * h_ref[...] + moe + ple, g_ref[...], b_ref[...])
    o_ref[...] = x
    ob_ref[...] = x.astype(BF16)


def layer_out(h, hb, p_prompt, p_sample, layer, route, pos, yb, w_ple_gate, w_ple_proj, ln_g, ln_b, tm=256):
    T, D = h.shape
    depth = w_ple_gate.shape[0] // D
    pt = p_prompt.shape[0] // depth // tm
    st = p_sample.shape[0] // depth // tm
    row = lambda n: pl.BlockSpec((tm, n), lambda i, pos: (i, 0))
    full = lambda a, b: pl.BlockSpec((a, b), lambda i, pos: (0, 0))
    side_p = pl.BlockSpec((tm, PLE_DIM), lambda i, pos: (layer * pt + jnp.minimum(i, pt - 1), 0))
    side_s = pl.BlockSpec((tm, PLE_DIM), lambda i, pos: (layer * st + jnp.clip(i - pt, 0, st - 1), 0))
    return pl.pallas_call(
        functools.partial(_layer_out_kernel, prompt_tiles=pt),
        grid_spec=pltpu.PrefetchScalarGridSpec(
            num_scalar_prefetch=1, grid=(T // tm,),
            in_specs=[row(D), row(D), side_p, side_s, row(LANES), pl.BlockSpec(memory_space=pl.ANY),
                      pl.BlockSpec((D, D), lambda i, pos: (layer, 0)), pl.BlockSpec((PLE_DIM, D), lambda i, pos: (layer, 0)),
                      full(1, D), full(1, D)],
            out_specs=[row(D), row(D)],
            scratch_shapes=[pltpu.VMEM((TOP_K, tm * TOK_TILE, LANES), F32), pltpu.SemaphoreType.DMA(())]),
        out_shape=[jax.ShapeDtypeStruct((T, D), F32), jax.ShapeDtypeStruct((T, D), BF16)],
        compiler_params=_cparams("arbitrary"), name="moe_combine_ple_ln2")(
            pos, h, hb, p_prompt, p_sample, route, yb, w_ple_gate, w_ple_proj, ln_g.reshape(1, D), ln_b.reshape(1, D))


def kernel(x_prompt, x_sample, p_prompt, p_sample, cache_k, cache_v, state_ssm_re, state_ssm_im, ln_in_g, ln_in_b, w_in, ssm_lam_re, ssm_lam_im, ssm_log_step, ssm_b_re, ssm_b_im, ssm_c_re, ssm_c_im, ssm_d, ssm_w_glu, ssm_b_glu, g_attn, g_ssm, w_out, ln1_g, ln1_b, w_router_group, b_router_group, w_router_expert, b_router_expert, w_exp_gate, w_exp_up, w_exp_down, w_ple_gate, w_ple_proj, ln2_g, ln2_b):
    B, L, D = x_prompt.shape
    SB, SL, _ = x_sample.shape
    depth = w_in.shape[0]
    past = cache_k.shape[2]
    TP, TS = B * L, SB * SL
    T = TP + TS
    n_blocks = (T * TOP_K) // MOE_ROWS + N_EXPERTS

    x, xb = ln_in(x_prompt.reshape(TP, D), x_sample.reshape(TS, D), ln_in_g, ln_in_b)
    p_p = p_prompt.reshape(depth * TP, PLE_DIM)
    p_s = p_sample.reshape(depth * TS, PLE_DIM)

    w_in_b = w_in.astype(BF16)
    w_out_b = w_out.astype(BF16)
    w_glu_b = ssm_w_glu.astype(BF16)
    w_pg_b = w_ple_gate.astype(BF16).reshape(depth * D, D)
    w_pp_b = w_ple_proj.astype(BF16).reshape(depth * PLE_DIM, D)
    w_route = jnp.pad(jnp.concatenate([w_router_group, w_router_expert], axis=-1),
                      ((0, 0), (0, 0), (0, LANES - N_EXPERT_GROUPS - N_EXPERTS)))
    b_route = jnp.pad(jnp.concatenate([b_router_group, b_router_expert], axis=-1),
                      ((0, 0), (0, LANES - N_EXPERT_GROUPS - N_EXPERTS))).reshape(depth, 1, LANES)
    a_re, a_im, bb_re, bb_im = ssm_params(ssm_lam_re, ssm_lam_im, ssm_log_step, ssm_b_re, ssm_b_im)

    cache_k_all = cache_k.reshape(depth * SB, past * N_HEADS, HEAD_DIM)
    cache_v_all = cache_v.reshape(depth * SB, past * N_HEADS, HEAD_DIM)
    w_eg = w_exp_gate.reshape(depth * N_EXPERTS, D, D_EXPERT)
    w_eu = w_exp_up.reshape(depth * N_EXPERTS, D, D_EXPERT)
    w_ed = w_exp_down.reshape(depth * N_EXPERTS, D_EXPERT, D)
    zeros_state = jnp.zeros((SUBLANES, SSM_LANES), F32)
    kv_bufs = tuple(jnp.zeros((depth * rows * N_HEADS, HEAD_DIM), F32) for rows in (TP, TP, TS, TS))
    states = [[] for _ in range(4)]
    for l in range(depth):
        proj, *kv_bufs = w_in_proj(xb, w_in_b[l], l, kv_bufs, TP)
        attn_p = attn_prompt(proj, B, L)
        attn_s = attn_sample(proj, cache_k_all, cache_v_all, l, TP, SB, SL)

        wb, wc_re, wc_im = ssm_weights(bb_re[l], bb_im[l], ssm_c_re[l], ssm_c_im[l])
        lam_r = a_re[l].reshape(1, SSM_LANES)
        lam_i = a_im[l].reshape(1, SSM_LANES)
        d_row = ssm_d[l].reshape(1, W_SSM)
        ssm_p, hr_p, hi_p = ssm_mixer(proj, 0, B, L, zeros_state, zeros_state, lam_r, lam_i, wb, wc_re, wc_im,
                                      d_row, w_glu_b[l], ssm_b_glu[l], steps=64, whole=False)
        ssm_s, hr_s, hi_s = ssm_mixer(proj, TP, SB, SL, state_ssm_re[l].reshape(SB, SSM_LANES),
                                      state_ssm_im[l].reshape(SB, SSM_LANES), lam_r, lam_i, wb, wc_re, wc_im,
                                      d_row, w_glu_b[l], ssm_b_glu[l], steps=8, whole=True)
        h, hb, h_tiles = mix_out(attn_p, attn_s, ssm_p.reshape(TP, W_SSM), ssm_s.reshape(TS, W_SSM), x,
                                 g_attn[l], g_ssm[l], w_out_b[l], ln1_g[l], ln1_b[l])
        route = router(h, w_route[l], b_route[l])
        slot_tok, pos, block_expert, n_used = dispatch_plan(route[:, :TOP_K].astype(jnp.int32), n_blocks)
        yb = expert_mlp(h_tiles, slot_tok, block_expert, n_used, w_eg, w_eu, w_ed, l)
        x, xb = layer_out(h, hb, p_p, p_s, l, route, pos, yb, w_pg_b, w_pp_b, ln2_g[l], ln2_b[l])

        G, P = N_SSM_GROUPS, SSM_STATE
        for lst, val in zip(states, (hr_p[:B].reshape(B, G, P), hi_p[:B].reshape(B, G, P),
                                     hr_s.reshape(SB, G, P), hi_s.reshape(SB, G, P))):
            lst.append(val)

    k_p, v_p, k_s, v_s = kv_bufs
    s_rp, s_ip, s_rs, s_is = (jnp.stack(s) for s in states)
    heads_p = (depth, B, L, N_HEADS, HEAD_DIM)
    heads_s = (depth, SB, SL, N_HEADS, HEAD_DIM)
    return (x[:TP].reshape(B, L, D), x[TP:].reshape(SB, SL, D), k_p.reshape(heads_p), v_p.reshape(heads_p),
            s_rp, s_ip, k_s.reshape(heads_s), v_s.reshape(heads_s), s_rs, s_is)
```

```python
import functools
import math

import jax
import jax.numpy as jnp
from jax import lax
from jax.experimental import pallas as pl
from jax.experimental.pallas import tpu as pltpu

F32 = jnp.float32
BF16 = jnp.bfloat16

D_MODEL = 2048
DEPTH = 4
W_ATTN = 1024
W_SSM = 1024
HEAD_DIM = 128
N_HEADS = 8
SSM_GROUP = 16
N_SSM_GROUPS = 64
SSM_STATE = 64
N_EXPERT_GROUPS = 4
EXPERTS_PER_GROUP = 8
N_EXPERTS = 32
TOP_K = 2
D_EXPERT = 512
PLE_DIM = 256
ALPHA = (2 * DEPTH) ** 0.25
LN_EPS = 1e-5
LOG2_E = math.log2(math.e)

LANES = 128
SUBLANES = 8
VMEM_LIMIT = 56 * 1024 * 1024
SSM_LANES = N_SSM_GROUPS * SSM_STATE
SSM_PIECES = 8
ATT_BLK = 128
ATT_TQ = 512
ATT_UNROLL = 4
MOE_ROWS = 256
TOK_TILE = D_MODEL // LANES
DMA_PRIORITIES = 2


def _cparams(*sem):
    return pltpu.CompilerParams(dimension_semantics=sem, vmem_limit_bytes=VMEM_LIMIT)


def _dot(a, b):
    return jnp.dot(a, b, preferred_element_type=F32)


def _dot_nt(a, b):
    return lax.dot_general(a, b, (((1,), (1,)), ((), ())), preferred_element_type=F32)


def _split_bf16(x):
    hi = x.astype(BF16)
    lo = (x - hi.astype(F32)).astype(BF16)
    return hi, lo


def _layer_norm(x, g, b):
    mu = jnp.mean(x, -1, keepdims=True)
    xc = x - mu
    var = jnp.mean(xc * xc, -1, keepdims=True)
    return xc * lax.rsqrt(var + LN_EPS) * g + b


def _rms_norm(x, g):
    return x * lax.rsqrt(jnp.mean(x * x, -1, keepdims=True) + LN_EPS) * g


def _sigmoid(x):
    return 1.0 / (1.0 + jnp.exp(-x))


def _ln_in_kernel(xp_ref, xs_ref, g_ref, b_ref, o_ref, ob_ref, *, prompt_tiles):
    x = jnp.where(pl.program_id(0) < prompt_tiles, xp_ref[...], xs_ref[...])
    y = _layer_norm(x, g_ref[...], b_ref[...])
    o_ref[...] = y
    ob_ref[...] = y.astype(BF16)


def ln_in(x_prompt, x_sample, g, b, tm=256):
    D = x_prompt.shape[1]
    pt, st = x_prompt.shape[0] // tm, x_sample.shape[0] // tm
    T = (pt + st) * tm
    row = pl.BlockSpec((tm, D), lambda i: (i, 0))
    vec = pl.BlockSpec((1, D), lambda i: (0, 0))
    return pl.pallas_call(
        functools.partial(_ln_in_kernel, prompt_tiles=pt), grid=(pt + st,),
        in_specs=[pl.BlockSpec((tm, D), lambda i: (jnp.minimum(i, pt - 1), 0)),
                  pl.BlockSpec((tm, D), lambda i: (jnp.clip(i - pt, 0, st - 1), 0)), vec, vec],
        out_specs=[row, row],
        out_shape=[jax.ShapeDtypeStruct((T, D), F32), jax.ShapeDtypeStruct((T, D), BF16)],
        compiler_params=_cparams("arbitrary"), name="ln_in")(x_prompt, x_sample, g.reshape(1, D), b.reshape(1, D))


def _w_in_kernel(x_ref, w_ref, kp_in, vp_in, ks_in, vs_in, o_ref, kp_ref, vp_ref, ks_ref, vs_ref, *, prompt_tiles):
    del kp_in, vp_in, ks_in, vs_in
    j, i = pl.program_id(0), pl.program_id(1)
    y = _dot(x_ref[...], w_ref[...])
    o_ref[...] = y

    def store_heads(ref):
        for h in range(N_HEADS):
            ref[pl.ds(h, y.shape[0], stride=N_HEADS), :] = y[:, h * HEAD_DIM:(h + 1) * HEAD_DIM]

    for col, p_ref, s_ref in ((1, kp_ref, ks_ref), (2, vp_ref, vs_ref)):
        @pl.when(jnp.logical_and(j == col, i < prompt_tiles))
        def _(p_ref=p_ref):
            store_heads(p_ref)

        @pl.when(jnp.logical_and(j == col, i >= prompt_tiles))
        def _(s_ref=s_ref):
            store_heads(s_ref)


def w_in_proj(x, w, layer, kv_bufs, prompt_rows, tm=512):
    M, K = x.shape
    tn = W_ATTN
    pt = prompt_rows // tm
    st = (M - prompt_rows) // tm
    def heads(col, n_tiles, first_tile):
        def idx(j, i):
            during = jnp.clip(i - first_tile, 0, n_tiles - 1)
            return (layer * n_tiles + jnp.where(j == col, during, jnp.where(j < col, 0, n_tiles - 1)), 0)
        return pl.BlockSpec((tm * N_HEADS, HEAD_DIM), idx)
    any_spec = pl.BlockSpec(memory_space=pl.ANY)
    return pl.pallas_call(
        functools.partial(_w_in_kernel, prompt_tiles=pt), grid=(w.shape[1] // tn, M // tm),
        in_specs=[pl.BlockSpec((tm, K), lambda j, i: (i, 0)), pl.BlockSpec((K, tn), lambda j, i: (0, j)),
                  any_spec, any_spec, any_spec, any_spec],
        out_specs=[pl.BlockSpec((tm, tn), lambda j, i: (i, j)),
                   heads(1, pt, 0), heads(2, pt, 0), heads(1, st, pt), heads(2, st, pt)],
        out_shape=[jax.ShapeDtypeStruct((M, w.shape[1]), F32)] + [jax.ShapeDtypeStruct(b.shape, F32) for b in kv_bufs],
        input_output_aliases={2: 1, 3: 2, 4: 3, 5: 4},
        compiler_params=_cparams("arbitrary", "arbitrary"), name="w_in_proj")(x, w, *kv_bufs)


def _suffix_sum_matrix(n, width):
    r = lax.broadcasted_iota(jnp.int32, (2 * n, 2 * width), 0)
    c = lax.broadcasted_iota(jnp.int32, (2 * n, 2 * width), 1)
    j = jnp.where(r >= n, r - n, r)
    later = jnp.logical_or(c < width, jnp.logical_and(c < width + n, j > c - width))
    return jnp.where(later, 1.0, 0.0).astype(BF16)


def _sb_weights(z, run, sums, n_vis, diag):
    rows, nk = z.shape
    bw = min(nk, ATT_BLK)
    nb = nk // bw
    blk = lambda a, b: a[:, b * bw:(b + 1) * bw]
    zs = [blk(z, b) for b in range(n_vis)]
    sps = [jnp.maximum(zb, 0.0) + jnp.log2(1.0 + jnp.exp2(-jnp.abs(zb))) for zb in zs]
    if diag:
        visible = (lax.broadcasted_iota(jnp.int32, (rows, bw), 1) < lax.broadcasted_iota(jnp.int32, (rows, bw), 0))
        sps[-1] = jnp.where(visible, sps[-1], 0.0)
    stacked = jnp.concatenate([jnp.concatenate(_split_bf16(s), axis=-1) for s in sps], axis=0)
    sums_all = _dot(stacked, sums)
    sums_of = [sums_all[b * rows:(b + 1) * rows] for b in range(n_vis)]
    ws = [jnp.zeros((rows, bw), BF16)] * nb
    for b in reversed(range(n_vis)):
        later = sums_of[b][:, LANES:LANES + bw]
        w = jnp.exp2(zs[b] - sps[b] - later - run[:, :bw])
        if diag and b == n_vis - 1:
            w = jnp.where(visible, w, 0.0)
        ws[b] = w.astype(BF16)
        run = run + sums_of[b][:, :LANES]
    return jnp.concatenate(ws, axis=-1), run


def _attn_prompt_kernel(q_ref, k_ref, v_ref, o_ref, kb_ref, vb_ref, acc_ref, run_ref, z_ref, w_ref):
    qi = pl.program_id(2)
    nsub = ATT_TQ // ATT_BLK

    @pl.when(qi == 0)
    def _():
        kb_ref[...] = k_ref[...].astype(BF16)
        vb_ref[...] = v_ref[...].astype(BF16)

    sums = _suffix_sum_matrix(ATT_BLK, LANES)
    qs = (q_ref[...] * (HEAD_DIM ** -0.5 * LOG2_E)).astype(BF16)
    sub = lambda s: slice(s * ATT_BLK, (s + 1) * ATT_BLK)

    def scores(chunk, s):
        start = pl.multiple_of(chunk * ATT_TQ, ATT_TQ)
        z_ref[s] = _dot_nt(qs[sub(s)], kb_ref[pl.ds(start, ATT_TQ), :])

    def values(chunk, s):
        start = pl.multiple_of(chunk * ATT_TQ, ATT_TQ)
        return _dot(w_ref[s], vb_ref[pl.ds(start, ATT_TQ), :])

    for s in range(nsub):
        scores(qi, s)

    for s in range(nsub):
        z = z_ref[s]
        scores(jnp.maximum(qi - 1, 0), s)
        w_ref[s], run_ref[sub(s), :] = _sb_weights(z, jnp.zeros((ATT_BLK, LANES), F32), sums, s + 1, True)
    acc_ref[...] = jnp.zeros_like(acc_ref)

    def body(it, c):
        for s in range(nsub):
            z = z_ref[s]
            scores(jnp.maximum(qi - 2 - it, 0), s)
            pv = values(qi - it, s)
            w_ref[s], run_ref[sub(s), :] = _sb_weights(z, run_ref[sub(s), :], sums, nsub, False)
            acc_ref[sub(s), :] += pv
        return c

    lax.fori_loop(0, qi, body, 0)
    for s in range(nsub):
        o_ref[sub(s), :] = acc_ref[sub(s), :] + values(0, s)


def attn_prompt(proj, batch, seq):
    nq = seq // ATT_TQ
    q_spec = pl.BlockSpec((ATT_TQ, HEAD_DIM), lambda b, h, i: (b * nq + i, h))
    k_spec = pl.BlockSpec((seq, HEAD_DIM), lambda b, h, i: (b, N_HEADS + h))
    v_spec = pl.BlockSpec((seq, HEAD_DIM), lambda b, h, i: (b, 2 * N_HEADS + h))
    return pl.pallas_call(
        _attn_prompt_kernel, grid=(batch, N_HEADS, nq), in_specs=[q_spec, k_spec, v_spec],
        out_specs=q_spec, out_shape=jax.ShapeDtypeStruct((batch * seq, W_ATTN), F32),
        scratch_shapes=[pltpu.VMEM((seq, HEAD_DIM), BF16), pltpu.VMEM((seq, HEAD_DIM), BF16),
                        pltpu.VMEM((ATT_TQ, HEAD_DIM), F32), pltpu.VMEM((ATT_TQ, LANES), F32),
                        pltpu.VMEM((ATT_TQ // ATT_BLK, ATT_BLK, ATT_TQ), F32),
                        pltpu.VMEM((ATT_TQ // ATT_BLK, ATT_BLK, ATT_TQ), BF16)],
        compiler_params=_cparams("parallel", "parallel", "arbitrary"), name="attn_prompt")(proj, proj, proj)


def _attn_sample_kernel(q_ref, k_ref, v_ref, ck_ref, cv_ref, o_ref, qs_ref, kn_ref, vn_ref, oh_ref, *, past, tq):
    for h in range(N_HEADS):
        cols = slice(h * HEAD_DIM, (h + 1) * HEAD_DIM)
        qs_ref[h] = (q_ref[:, cols] * (HEAD_DIM ** -0.5 * LOG2_E)).astype(BF16)
        kn_ref[h] = k_ref[:, cols].astype(BF16)
        vn_ref[h] = v_ref[:, cols].astype(BF16)
    sums_new = _suffix_sum_matrix(tq, LANES)
    sums = _suffix_sum_matrix(ATT_BLK, LANES)

    def head(h):
        qs = qs_ref[h]
        w_new, run = _sb_weights(_dot_nt(qs, kn_ref[h]), jnp.zeros((tq, LANES), F32), sums_new, 1, True)
        cached = pl.ds(h, past, stride=N_HEADS)
        z = _dot_nt(qs, ck_ref[cached, :].astype(BF16))
        w_old, _ = _sb_weights(z, run, sums, past // ATT_BLK, False)
        oh_ref[h] = _dot(w_new, vn_ref[h]) + _dot(w_old, cv_ref[cached, :].astype(BF16))

    def head_pair(i, c):
        head(2 * i)
        head(2 * i + 1)
        return c

    lax.fori_loop(0, N_HEADS // 2, head_pair, 0)
    for h in range(N_HEADS):
        o_ref[:, h * HEAD_DIM:(h + 1) * HEAD_DIM] = oh_ref[h]


def attn_sample(proj, cache_k, cache_v, layer, row0, batch, tq):
    past = cache_k.shape[1] // N_HEADS
    rb0 = row0 // tq
    new = lambda c: pl.BlockSpec((tq, W_ATTN), lambda b: (rb0 + b, c))
    cache = pl.BlockSpec((None, past * N_HEADS, HEAD_DIM), lambda b: (layer * batch + b, 0, 0))
    per_head = lambda dt: pltpu.VMEM((N_HEADS, tq, HEAD_DIM), dt)
    return pl.pallas_call(
        functools.partial(_attn_sample_kernel, past=past, tq=tq), grid=(batch,),
        in_specs=[new(0), new(1), new(2), cache, cache],
        out_specs=pl.BlockSpec((tq, W_ATTN), lambda b: (b, 0)),
        out_shape=jax.ShapeDtypeStruct((batch * tq, W_ATTN), F32),
        scratch_shapes=[per_head(BF16), per_head(BF16), per_head(BF16), per_head(F32)],
        compiler_params=_cparams("parallel"), name="attn_sample")(proj, proj, proj, cache_k, cache_v)


def _ssm_param_kernel(lr_ref, li_ref, ls_ref, br_ref, bi_ref, ar_ref, ai_ref, bbr_ref, bbi_ref):
    lam_re = jnp.minimum(lr_ref[0], -1e-4)
    lam_im = li_ref[0]
    step = jnp.exp(ls_ref[0])
    mag = jnp.exp(lam_re * step)
    a_re = mag * jnp.cos(lam_im * step)
    a_im = mag * jnp.sin(lam_im * step)
    ar_ref[0] = a_re
    ai_ref[0] = a_im
    den = lam_re * lam_re + lam_im * lam_im
    nr = a_re - 1.0
    c_re = (nr * lam_re + a_im * lam_im) / den
    c_im = (a_im * lam_re - nr * lam_im) / den
    for c in range(SSM_GROUP):
        b_re = br_ref[0, c]
        b_im = bi_ref[0, c]
        bbr_ref[0, c] = c_re * b_re - c_im * b_im
        bbi_ref[0, c] = c_re * b_im + c_im * b_re


def ssm_params(lam_re, lam_im, log_step, b_re, b_im):
    G, P, C = N_SSM_GROUPS, SSM_STATE, SSM_GROUP
    gp = pl.BlockSpec((1, G, P), lambda l: (l, 0, 0))
    cgp = pl.BlockSpec((1, C, G, P), lambda l: (l, 0, 0, 0))
    return pl.pallas_call(
        _ssm_param_kernel, grid=(DEPTH,),
        in_specs=[gp, gp, pl.BlockSpec((1, G, 1), lambda l: (l, 0, 0)), cgp, cgp],
        out_specs=[gp, gp, cgp, cgp],
        out_shape=[jax.ShapeDtypeStruct((DEPTH, G, P), F32)] * 2 + [jax.ShapeDtypeStruct((DEPTH, C, G, P), F32)] * 2,
        compiler_params=_cparams("parallel"), name="ssm_params")(
            lam_re, lam_im, log_step.reshape(DEPTH, G, 1),
            b_re.transpose(0, 3, 1, 2), b_im.transpose(0, 3, 1, 2))


def _gelu_tanh(y):
    return 0.5 * y * (1.0 + jnp.tanh(math.sqrt(2.0 / math.pi) * (y + 0.044715 * (y * y * y))))


def _ssm_kernel(*refs, n_seq, rows, steps, seq_len, whole):
    n_io = 1 if whole else n_seq
    u_refs = refs[:n_io]
    (h0r_ref, h0i_ref, ar_ref, ai_ref, wb_ref, wcr_ref, wci_ref, d_ref, wg_ref, bg_ref) = refs[n_io:n_io + 10]
    o_ref, sr_ref, si_ref, us_ref, hr_ref, hi_ref, zs_ref = refs[n_io + 10:]
    t = pl.program_id(0)
    piece_in = SSM_LANES // SSM_PIECES
    piece_ch = W_SSM // SSM_PIECES
    lane_blk = 4 * SUBLANES * LANES // rows

    def seq_rows(b):
        return pl.ds(pl.multiple_of(b * seq_len + t * steps, steps), steps)

    @pl.when(t == 0)
    def _():
        sr_ref[...] = h0r_ref[...]
        si_ref[...] = h0i_ref[...]
        if n_seq < rows:
            us_ref[...] = jnp.zeros_like(us_ref)

    for b in range(n_seq):
        ub = u_refs[0][seq_rows(b), :] if whole else u_refs[b][...]
        for k in range(SSM_PIECES):
            us_ref[k, pl.ds(b, steps, stride=rows), :] = ub[:, k * piece_ch:(k + 1) * piece_ch]

    for k in range(SSM_PIECES):
        bu = _dot(us_ref[k].astype(BF16), wb_ref[k])
        hr_ref[:, k * piece_in:(k + 1) * piece_in] = bu[:, :piece_in]
        hi_ref[:, k * piece_in:(k + 1) * piece_in] = bu[:, piece_in:]

    for lb in range(SSM_LANES // lane_blk):
        ln = slice(lb * lane_blk, (lb + 1) * lane_blk)
        a_re = jnp.broadcast_to(ar_ref[:, ln], (rows, lane_blk))
        a_im = jnp.broadcast_to(ai_ref[:, ln], (rows, lane_blk))

        def step(j, carry, ln=ln, a_re=a_re, a_im=a_im):
            h_re, h_im = carry
            r0 = pl.multiple_of(j * rows, rows)
            n_re = a_re * h_re - a_im * h_im + hr_ref[pl.ds(r0, rows), ln]
            n_im = a_re * h_im + a_im * h_re + hi_ref[pl.ds(r0, rows), ln]
            hr_ref[pl.ds(r0, rows), ln] = n_re
            hi_ref[pl.ds(r0, rows), ln] = n_im
            return n_re, n_im

        h_re, h_im = lax.fori_loop(0, steps, step, (sr_ref[:, ln], si_ref[:, ln]), unroll=4)
        sr_ref[:, ln] = h_re
        si_ref[:, ln] = h_im

    zs = []
    for k in range(SSM_PIECES):
        ln = slice(k * piece_in, (k + 1) * piece_in)
        y = _dot(hr_ref[:, ln].astype(BF16), wcr_ref[k]) - _dot(hi_ref[:, ln].astype(BF16), wci_ref[k])
        zs.append(_gelu_tanh(y + d_ref[:, k * piece_ch:(k + 1) * piece_ch] * us_ref[k]))
    z = jnp.concatenate(zs, axis=-1)
    out = z * _sigmoid(_dot(z.astype(BF16), wg_ref[...]) + bg_ref[...])
    for k in range(SSM_PIECES):
        zs_ref[k] = out[:, k * piece_ch:(k + 1) * piece_ch]

    for b in range(n_seq):
        ob = jnp.concatenate([zs_ref[k, pl.ds(b, steps, stride=rows), :] for k in range(SSM_PIECES)], axis=-1)
        if whole:
            o_ref[seq_rows(b), :] = ob
        else:
            o_ref[b] = ob


def ssm_mixer(proj, row0, n_seq, seq_len, h0_re, h0_im, a_re, a_im, wb, wc_re, wc_im, d, w_glu, b_glu, steps, whole):
    rows = h0_re.shape[0]
    tile = rows * steps
    u_col = proj.shape[1] // W_SSM - 1
    full = lambda *s: pl.BlockSpec(s, lambda t: (0,) * len(s))
    if whole:
        u_specs = [pl.BlockSpec((n_seq * seq_len, W_SSM), lambda t: (row0 // (n_seq * seq_len), u_col))]
        o_spec = full(n_seq * seq_len, W_SSM)
        o_shape = jax.ShapeDtypeStruct((n_seq * seq_len, W_SSM), F32)
    else:
        u_specs = [pl.BlockSpec((steps, W_SSM), lambda t, b=b: (b * (seq_len // steps) + t, u_col)) for b in range(n_seq)]
        o_spec = pl.BlockSpec((n_seq, steps, W_SSM), lambda t: (0, t, 0))
        o_shape = jax.ShapeDtypeStruct((n_seq, seq_len, W_SSM), F32)
    state = full(rows, SSM_LANES)
    out, s_re, s_im = pl.pallas_call(
        functools.partial(_ssm_kernel, n_seq=n_seq, rows=rows, steps=steps, seq_len=seq_len, whole=whole),
        grid=(seq_len // steps,),
        in_specs=u_specs + [state, state, full(1, SSM_LANES), full(1, SSM_LANES), full(*wb.shape),
                            full(*wc_re.shape), full(*wc_im.shape), full(1, W_SSM), full(W_SSM, W_SSM), full(1, W_SSM)],
        out_specs=[o_spec, state, state],
        out_shape=[o_shape] + [jax.ShapeDtypeStruct((rows, SSM_LANES), F32)] * 2,
        scratch_shapes=[pltpu.VMEM((SSM_PIECES, tile, LANES), F32), pltpu.VMEM((tile, SSM_LANES), F32),
                        pltpu.VMEM((tile, SSM_LANES), F32), pltpu.VMEM((SSM_PIECES, tile, LANES), F32)],
        compiler_params=_cparams("arbitrary"), name="ssm_mixer")(
            *([proj] * len(u_specs)), h0_re, h0_im, a_re, a_im, wb, wc_re, wc_im, d, w_glu, b_glu.reshape(1, W_SSM))
    return out.reshape(n_seq, seq_len, W_SSM), s_re, s_im


def _block_diag(w):
    k, g, a, b = w.shape
    eye = jnp.eye(g, dtype=w.dtype)
    return (w[:, :, :, None, :] * eye[None, :, None, :, None]).reshape(k, g * a, g * b)


def ssm_weights(bb_re, bb_im, c_re, c_im):
    gpp = N_SSM_GROUPS // SSM_PIECES
    to_in = lambda b: b.transpose(1, 0, 2).reshape(SSM_PIECES, gpp, SSM_GROUP, SSM_STATE)
    wb = jnp.concatenate([_block_diag(to_in(bb_re)), _block_diag(to_in(bb_im))], axis=-1)
    to_out = lambda c: c.transpose(0, 2, 1).reshape(SSM_PIECES, gpp, SSM_STATE, SSM_GROUP)
    return wb.astype(BF16), _block_diag(to_out(c_re)).astype(BF16), _block_diag(to_out(c_im)).astype(BF16)


def _store_token_tiles(ref, x):
    for c in range(TOK_TILE):
        ref[pl.ds(c, x.shape[0], stride=TOK_TILE), :] = x[:, c * LANES:(c + 1) * LANES]


def _load_token_tiles(ref, rows):
    return jnp.concatenate([ref[pl.ds(c, rows, stride=TOK_TILE), :] for c in range(TOK_TILE)], axis=-1)


def _mix_out_kernel(ap_ref, as_ref, sp_ref, ss_ref, x_ref, ga_ref, gs_ref, w_ref, g_ref, b_ref,
                    h_ref, hb_ref, ht_ref, *, prompt_tiles):
    is_prompt = pl.program_id(0) < prompt_tiles
    a = _rms_norm(jnp.where(is_prompt, ap_ref[...], as_ref[...]), ga_ref[...]).astype(BF16)
    s = _rms_norm(jnp.where(is_prompt, sp_ref[...], ss_ref[...]), gs_ref[...]).astype(BF16)
    y = _dot(a, w_ref[:W_ATTN, :]) + _dot(s, w_ref[W_ATTN:, :])
    h = _layer_norm(ALPHA * x_ref[...] + y, g_ref[...], b_ref[...])
    h_ref[...] = h
    hb_ref[...] = h.astype(BF16)
    _store_token_tiles(ht_ref, h)


def mix_out(attn_p, attn_s, ssm_p, ssm_s, x, g_attn, g_ssm, w_out, ln_g, ln_b, tm=256):
    T, D = x.shape
    pt = attn_p.shape[0] // tm
    st = attn_s.shape[0] // tm
    prompt = pl.BlockSpec((tm, W_ATTN), lambda i: (jnp.minimum(i, pt - 1), 0))
    sample = pl.BlockSpec((tm, W_ATTN), lambda i: (jnp.clip(i - pt, 0, st - 1), 0))
    row = pl.BlockSpec((tm, D), lambda i: (i, 0))
    vec = lambda n: pl.BlockSpec((1, n), lambda i: (0, 0))
    return pl.pallas_call(
        functools.partial(_mix_out_kernel, prompt_tiles=pt), grid=(T // tm,),
        in_specs=[prompt, sample, prompt, sample, row, vec(W_ATTN), vec(W_SSM),
                  pl.BlockSpec((D, D), lambda i: (0, 0)), vec(D), vec(D)],
        out_specs=[row, row, pl.BlockSpec((tm * TOK_TILE, LANES), lambda i: (i, 0))],
        out_shape=[jax.ShapeDtypeStruct((T, D), F32), jax.ShapeDtypeStruct((T, D), BF16),
                   jax.ShapeDtypeStruct((T * TOK_TILE, LANES), F32)],
        compiler_params=_cparams("arbitrary"), name="mix_out_ln1")(
            attn_p, attn_s, ssm_p, ssm_s, x, g_attn.reshape(1, -1), g_ssm.reshape(1, -1), w_out,
            ln_g.reshape(1, D), ln_b.reshape(1, D))


def _router_kernel(h_ref, w_ref, b_ref, o_ref):
    tm = h_ref.shape[0]
    hh, hl = _split_bf16(h_ref[...])
    wh, wl = _split_bf16(w_ref[...])
    logits = _dot(hh, wh) + _dot(hh, wl) + _dot(hl, wh) + b_ref[...]
    lane = lax.broadcasted_iota(jnp.int32, (tm, LANES), 1).astype(F32)
    neg = -jnp.inf
    big = float(LANES)

    def first_max(v):
        m = jnp.max(v, -1, keepdims=True)
        return m, jnp.min(jnp.where(v == m, lane, big), -1, keepdims=True)

    is_group = lane < N_EXPERT_GROUPS
    g_max, g_idx = first_max(jnp.where(is_group, logits, neg))
    pg_top = 1.0 / jnp.sum(jnp.where(is_group, jnp.exp(logits - g_max), 0.0), -1, keepdims=True)
    lo = N_EXPERT_GROUPS + EXPERTS_PER_GROUP * g_idx
    le = jnp.where(lane >= lo, jnp.where(lane < lo + EXPERTS_PER_GROUP, logits, neg), neg)
    m1, i1 = first_max(le)
    m2, i2 = first_max(jnp.where(lane == i1, neg, le))
    e2 = jnp.exp(m2 - m1)
    den = 1.0 + e2
    o_ref[...] = jnp.where(lane == 0, i1 - N_EXPERT_GROUPS,
                 jnp.where(lane == 1, i2 - N_EXPERT_GROUPS,
                 jnp.where(lane == 2, pg_top / den,
                 jnp.where(lane == 3, pg_top * e2 / den, 0.0))))


def router(h, w_cat, b_cat, tm=512):
    T, D = h.shape
    return pl.pallas_call(
        _router_kernel, grid=(T // tm,),
        in_specs=[pl.BlockSpec((tm, D), lambda i: (i, 0)), pl.BlockSpec((D, LANES), lambda i: (0, 0)),
                  pl.BlockSpec((1, LANES), lambda i: (0, 0))],
        out_specs=pl.BlockSpec((tm, LANES), lambda i: (i, 0)),
        out_shape=jax.ShapeDtypeStruct((T, LANES), F32),
        compiler_params=_cparams("parallel"), name="router")(h, w_cat, b_cat)


def dispatch_plan(expert_ids, n_blocks):
    flat = expert_ids.reshape(-1)
    A = flat.shape[0]
    onehot = (flat[:, None] == jnp.arange(N_EXPERTS, dtype=jnp.int32)[None, :]).reshape(A // LANES, LANES, N_EXPERTS)
    lower = (jnp.arange(LANES)[:, None] >= jnp.arange(LANES)[None, :]).astype(BF16)
    within = jnp.einsum('ij,cje->cie', lower, onehot.astype(BF16), preferred_element_type=F32)
    chunk_total = within[:, -1, :]
    chunk_start = jnp.cumsum(chunk_total, axis=0) - chunk_total
    rank = jnp.sum(jnp.where(onehot, within - 1.0 + chunk_start[:, None, :], 0.0), axis=-1).reshape(A).astype(jnp.int32)
    counts = (chunk_start[-1] + chunk_total[-1]).astype(jnp.int32)
    blocks = (counts + MOE_ROWS - 1) // MOE_ROWS
    blk_end = jnp.cumsum(blocks)
    pad_start = (blk_end - blocks) * MOE_ROWS
    pos = pad_start[flat] + rank
    slot_tok = jnp.zeros((n_blocks * MOE_ROWS,), jnp.int32).at[pos].set(jnp.arange(A, dtype=jnp.int32) // TOP_K)
    n_used = blk_end[-1]
    blk = jnp.minimum(jnp.arange(n_blocks, dtype=jnp.int32), n_used - 1)
    block_expert = jnp.minimum(jnp.searchsorted(blk_end, blk, side='right'), N_EXPERTS - 1).astype(jnp.int32)
    return slot_tok, pos.astype(jnp.int32), block_expert, n_used.reshape(1).astype(jnp.int32)


def _tile_copy(src_ref, dst_ref, src_tok, dst_tok, sem):
    return pltpu.make_async_copy(src_ref.at[pl.ds(src_tok * TOK_TILE, TOK_TILE)],
                                 dst_ref.at[pl.ds(dst_tok * TOK_TILE, TOK_TILE)], sem)


def _expert_kernel(tok_ref, be_ref, nu_ref, ht_ref, wg_ref, wu_ref, wd_ref, o_ref,
                   xbuf_ref, wgb_ref, wub_ref, wdb_ref, sem):
    i = pl.program_id(0)
    n_used = nu_ref[0]
    slot = lax.rem(i, 2)

    def start_gather(block, s):
        def body(r2, c):
            for prio in range(DMA_PRIORITIES):
                r = r2 * DMA_PRIORITIES + prio
                _tile_copy(ht_ref, xbuf_ref.at[s], tok_ref[block * MOE_ROWS + r], r, sem.at[s]).start(priority=prio)
            return c
        lax.fori_loop(0, MOE_ROWS // DMA_PRIORITIES, body, 0, unroll=4)

    def wait_gather(s):
        def body(r, c):
            _tile_copy(ht_ref, xbuf_ref.at[s], 0, r, sem.at[s]).wait()
            return c
        lax.fori_loop(0, MOE_ROWS, body, 0, unroll=8)

    @pl.when(i == 0)
    def _():
        start_gather(0, 0)

    @pl.when(i + 1 < n_used)
    def _():
        start_gather(i + 1, 1 - slot)

    @pl.when(jnp.logical_or(i == 0, be_ref[i] != be_ref[jnp.maximum(i - 1, 0)]))
    def _():
        wgb_ref[...] = wg_ref[0].astype(BF16)
        wub_ref[...] = wu_ref[0].astype(BF16)
        wdb_ref[...] = wd_ref[0].astype(BF16)

    @pl.when(i < n_used)
    def _():
        wait_gather(slot)
        x = _load_token_tiles(xbuf_ref.at[slot], MOE_ROWS).astype(BF16)
        g = _dot(x, wgb_ref[...])
        u = _dot(x, wub_ref[...])
        hid = (g * _sigmoid(g)) * u
        _store_token_tiles(o_ref, _dot(hid.astype(BF16), wdb_ref[...]))

    @pl.when(i >= n_used)
    def _():
        o_ref[...] = jnp.zeros_like(o_ref)


def expert_mlp(h_tiles, slot_tok, block_expert, n_used, w_gate, w_up, w_down, layer):
    nb = block_expert.shape[0]
    blk_rows = MOE_ROWS * TOK_TILE
    D = D_MODEL
    weight = lambda a, b: pl.BlockSpec((1, a, b), lambda i, tok, be, nu: (layer * N_EXPERTS + be[i], 0, 0))
    return pl.pallas_call(
        _expert_kernel,
        grid_spec=pltpu.PrefetchScalarGridSpec(
            num_scalar_prefetch=3, grid=(nb,),
            in_specs=[pl.BlockSpec(memory_space=pl.ANY), weight(D, D_EXPERT), weight(D, D_EXPERT), weight(D_EXPERT, D)],
            out_specs=pl.BlockSpec((blk_rows, LANES), lambda i, tok, be, nu: (i, 0)),
            scratch_shapes=[pltpu.VMEM((2, blk_rows, LANES), F32), pltpu.VMEM((D, D_EXPERT), BF16),
                            pltpu.VMEM((D, D_EXPERT), BF16), pltpu.VMEM((D_EXPERT, D), BF16),
                            pltpu.SemaphoreType.DMA((2,))]),
        out_shape=jax.ShapeDtypeStruct((nb * blk_rows, LANES), F32),
        compiler_params=_cparams("arbitrary"), name="moe_experts")(
            slot_tok, block_expert, n_used, h_tiles, w_gate, w_up, w_down)


def _layer_out_kernel(pos_ref, h_ref, hb_ref, pp_ref, ps_ref, r_ref, yb_ref, wg_ref, wp_ref, g_ref, b_ref,
                      o_ref, ob_ref, buf_ref, sem, *, prompt_tiles):
    tm = h_ref.shape[0]
    base = pl.program_id(0) * tm
    p = jnp.where(pl.program_id(0) < prompt_tiles, pp_ref[...], ps_ref[...])

    def start(r, c):
        for k in range(TOP_K):
            _tile_copy(yb_ref, buf_ref.at[k], pos_ref[(base + r) * TOP_K + k], r, sem).start(priority=k % DMA_PRIORITIES)
        return c

    def wait(r, c):
        for k in range(TOP_K):
            _tile_copy(yb_ref, buf_ref.at[k], 0, r, sem).wait()
        return c

    lax.fori_loop(0, tm, start, 0, unroll=8)
    ple = _sigmoid(_dot(hb_ref[...], wg_ref[...])) * _dot(p.astype(BF16), wp_ref[...])
    lax.fori_loop(0, tm, wait, 0, unroll=8)
    route = r_ref[...]
    moe = (route[:, 2:3] * _load_token_tiles(buf_ref.at[0], tm)
           + route[:, 3:4] * _load_token_tiles(buf_ref.at[1], tm))
    x = _layer_norm(ALPHA * h_ref[...] + moe + ple, g_ref[...], b_ref[...])
    o_ref[...] = x
    ob_ref[...] = x.astype(BF16)


def layer_out(h, hb, p_prompt, p_sample, layer, route, pos, yb, w_ple_gate, w_ple_proj, ln_g, ln_b, tm=256):
    T, D = h.shape
    depth = w_ple_gate.shape[0] // D
    pt = p_prompt.shape[0] // depth // tm
    st = p_sample.shape[0] // depth // tm
    row = lambda n: pl.BlockSpec((tm, n), lambda i, pos: (i, 0))
    full = lambda a, b: pl.BlockSpec((a, b), lambda i, pos: (0, 0))
    side_p = pl.BlockSpec((tm, PLE_DIM), lambda i, pos: (layer * pt + jnp.minimum(i, pt - 1), 0))
    side_s = pl.BlockSpec((tm, PLE_DIM), lambda i, pos: (layer * st + jnp.clip(i - pt, 0, st - 1), 0))
    return pl.pallas_call(
        functools.partial(_layer_out_kernel, prompt_tiles=pt),
        grid_spec=pltpu.PrefetchScalarGridSpec(
            num_scalar_prefetch=1, grid=(T // tm,),
            in_specs=[row(D), row(D), side_p, side_s, row(LANES), pl.BlockSpec(memory_space=pl.ANY),
                      pl.BlockSpec((D, D), lambda i, pos: (layer, 0)), pl.BlockSpec((PLE_DIM, D), lambda i, pos: (layer, 0)),
                      full(1, D), full(1, D)],
            out_specs=[row(D), row(D)],
            scratch_shapes=[pltpu.VMEM((TOP_K, tm * TOK_TILE, LANES), F32), pltpu.SemaphoreType.DMA(())]),
        out_shape=[jax.ShapeDtypeStruct((T, D), F32), jax.ShapeDtypeStruct((T, D), BF16)],
        compiler_params=_cparams("arbitrary"), name="moe_combine_ple_ln2")(
            pos, h, hb, p_prompt, p_sample, route, yb, w_ple_gate, w_ple_proj, ln_g.reshape(1, D), ln_b.reshape(1, D))


def kernel(x_prompt, x_sample, p_prompt, p_sample, cache_k, cache_v, state_ssm_re, state_ssm_im, ln_in_g, ln_in_b, w_in, ssm_lam_re, ssm_lam_im, ssm_log_step, ssm_b_re, ssm_b_im, ssm_c_re, ssm_c_im, ssm_d, ssm_w_glu, ssm_b_glu, g_attn, g_ssm, w_out, ln1_g, ln1_b, w_router_group, b_router_group, w_router_expert, b_router_expert, w_exp_gate, w_exp_up, w_exp_down, w_ple_gate, w_ple_proj, ln2_g, ln2_b):
    B, L, D = x_prompt.shape
    SB, SL, _ = x_sample.shape
    depth = w_in.shape[0]
    past = cache_k.shape[2]
    TP, TS = B * L, SB * SL
    T = TP + TS
    n_blocks = (T * TOP_K) // MOE_ROWS + N_EXPERTS

    x, xb = ln_in(x_prompt.reshape(TP, D), x_sample.reshape(TS, D), ln_in_g, ln_in_b)
    p_p = p_prompt.reshape(depth * TP, PLE_DIM)
    p_s = p_sample.reshape(depth * TS, PLE_DIM)

    w_in_b = w_in.astype(BF16)
    w_out_b = w_out.astype(BF16)
    w_glu_b = ssm_w_glu.astype(BF16)
    w_pg_b = w_ple_gate.astype(BF16).reshape(depth * D, D)
    w_pp_b = w_ple_proj.astype(BF16).reshape(depth * PLE_DIM, D)
    w_route = jnp.pad(jnp.concatenate([w_router_group, w_router_expert], axis=-1),
                      ((0, 0), (0, 0), (0, LANES - N_EXPERT_GROUPS - N_EXPERTS)))
    b_route = jnp.pad(jnp.concatenate([b_router_group, b_router_expert], axis=-1),
                      ((0, 0), (0, LANES - N_EXPERT_GROUPS - N_EXPERTS))).reshape(depth, 1, LANES)
    a_re, a_im, bb_re, bb_im = ssm_params(ssm_lam_re, ssm_lam_im, ssm_log_step, ssm_b_re, ssm_b_im)

    cache_k_all = cache_k.reshape(depth * SB, past * N_HEADS, HEAD_DIM)
    cache_v_all = cache_v.reshape(depth * SB, past * N_HEADS, HEAD_DIM)
    w_eg = w_exp_gate.reshape(depth * N_EXPERTS, D, D_EXPERT)
    w_eu = w_exp_up.reshape(depth * N_EXPERTS, D, D_EXPERT)
    w_ed = w_exp_down.reshape(depth * N_EXPERTS, D_EXPERT, D)
    zeros_state = jnp.zeros((SUBLANES, SSM_LANES), F32)
    kv_bufs = tuple(jnp.zeros((depth * rows * N_HEADS, HEAD_DIM), F32) for rows in (TP, TP, TS, TS))
    states = [[] for _ in range(4)]
    for l in range(depth):
        proj, *kv_bufs = w_in_proj(xb, w_in_b[l], l, kv_bufs, TP)
        attn_p = attn_prompt(proj, B, L)
        attn_s = attn_sample(proj, cache_k_all, cache_v_all, l, TP, SB, SL)

        wb, wc_re, wc_im = ssm_weights(bb_re[l], bb_im[l], ssm_c_re[l], ssm_c_im[l])
        lam_r = a_re[l].reshape(1, SSM_LANES)
        lam_i = a_im[l].reshape(1, SSM_LANES)
        d_row = ssm_d[l].reshape(1, W_SSM)
        ssm_p, hr_p, hi_p = ssm_mixer(proj, 0, B, L, zeros_state, zeros_state, lam_r, lam_i, wb, wc_re, wc_im,
                                      d_row, w_glu_b[l], ssm_b_glu[l], steps=64, whole=False)
        ssm_s, hr_s, hi_s = ssm_mixer(proj, TP, SB, SL, state_ssm_re[l].reshape(SB, SSM_LANES),
                                      state_ssm_im[l].reshape(SB, SSM_LANES), lam_r, lam_i, wb, wc_re, wc_im,
                                      d_row, w_glu_b[l], ssm_b_glu[l], steps=8, whole=True)
        h, hb, h_tiles = mix_out(attn_p, attn_s, ssm_p.reshape(TP, W_SSM), ssm_s.reshape(TS, W_SSM), x,
                                 g_attn[l], g_ssm[l], w_out_b[l], ln1_g[l], ln1_b[l])
        route = router(h, w_route[l], b_route[l])
        slot_tok, pos, block_expert, n_used = dispatch_plan(route[:, :TOP_K].astype(jnp.int32), n_blocks)
        yb = expert_mlp(h_tiles, slot_tok, block_expert, n_used, w_eg, w_eu, w_ed, l)
        x, xb = layer_out(h, hb, p_p, p_s, l, route, pos, yb, w_pg_b, w_pp_b, ln2_g[l], ln2_b[l])

        G, P = N_SSM_GROUPS, SSM_STATE
        for lst, val in zip(states, (hr_p[:B].reshape(B, G, P), hi_p[:B].reshape(B, G, P),
                                     hr_s.reshape(SB, G, P), hi_s.reshape(SB, G, P))):
            lst.append(val)

    k_p, v_p, k_s, v_s = kv_bufs
    s_rp, s_ip, s_rs, s_is = (jnp.stack(s) for s in states)
    heads_p = (depth, B, L, N_HEADS, HEAD_DIM)
    heads_s = (depth, SB, SL, N_HEADS, HEAD_DIM)
    return (x[:TP].reshape(B, L, D), x[TP:].reshape(SB, SL, D), k_p.reshape(heads_p), v_p.reshape(heads_p),
            s_rp, s_ip, k_s.reshape(heads_s), v_s.reshape(heads_s), s_rs, s_is)
```
